```python
import math
import jax, jax.numpy as jnp
from jax import lax
import numpy as np

D_MODEL = 1024
BATCH = 2
SEQ = 16384
DEPTH = 1
DEC_BATCH = 16
DEC_SEQ = 16
PAST_LEN = 2048

CHUNK = 64
D_SSM = D_MODEL // 2
SSM_GROUP = 16
N_SSM_GROUPS = D_SSM // SSM_GROUP
SSM_STATE = 64
D_CONV = D_MODEL - D_SSM
CONV_WIDTH = 31
N_MEM = 256
MEM_HEADS = 4
MEM_HEAD_DIM = D_MODEL // MEM_HEADS
D_FF = 4 * D_MODEL
D_IN = D_SSM + 2 * D_CONV
DT_MIN = 1e-3
DT_MAX = 1e-1
LN_EPS = 1e-5
ALPHA = (2.0 * DEPTH) ** 0.25
BETA = (8.0 * DEPTH) ** -0.25

kernel_name = "hybrid_s5_conformer_stream_step"


def layer_norm(x, g, b):
    xf = x.astype(jnp.float32)
    mu = jnp.mean(xf, axis=-1, keepdims=True)
    var = jnp.mean(jnp.square(xf - mu), axis=-1, keepdims=True)
    y = (xf - mu) * lax.rsqrt(var + LN_EPS) * g.astype(jnp.float32) + b.astype(jnp.float32)
    return y.astype(x.dtype)


def s5_discretize(a_re, a_im, log_dt, b_re, b_im):
    f32 = jnp.float32
    a_re = a_re.astype(f32)
    a_im = a_im.astype(f32)
    b_re = b_re.astype(f32)
    b_im = b_im.astype(f32)
    dt = jnp.exp(log_dt.astype(f32))[:, None]
    mag = jnp.exp(a_re * dt)
    ang = a_im * dt
    ab_re = mag * jnp.cos(ang)
    ab_im = mag * jnp.sin(ang)
    den = a_re * a_re + a_im * a_im
    p = ab_re - 1.0
    q = ab_im
    c_re = ((p * a_re + q * a_im) / den)[..., None]
    c_im = ((q * a_re - p * a_im) / den)[..., None]
    bb_re = c_re * b_re - c_im * b_im
    bb_im = c_re * b_im + c_im * b_re
    return ab_re, ab_im, bb_re, bb_im


def s5_block(h0_re, h0_im, u, ab_re, ab_im, bb_re, bb_im, c_re, c_im):
    bu_re = jnp.einsum("gnp,btgp->btgn", bb_re, u)
    bu_im = jnp.einsum("gnp,btgp->btgn", bb_im, u)
    a_re = jnp.broadcast_to(ab_re, bu_re.shape)
    a_im = jnp.broadcast_to(ab_im, bu_im.shape)

    def combine(e1, e2):
        a1r, a1i, b1r, b1i = e1
        a2r, a2i, b2r, b2i = e2
        return (a2r * a1r - a2i * a1i,
                a2r * a1i + a2i * a1r,
                a2r * b1r - a2i * b1i + b2r,
                a2r * b1i + a2i * b1r + b2i)

    pw_re, pw_im, hz_re, hz_im = lax.associative_scan(combine, (a_re, a_im, bu_re, bu_im), axis=1)
    h_re = hz_re + pw_re * h0_re[:, None] - pw_im * h0_im[:, None]
    h_im = hz_im + pw_re * h0_im[:, None] + pw_im * h0_re[:, None]
    y = (jnp.einsum("gpn,btgn->btgp", c_re, h_re)
         - jnp.einsum("gpn,btgn->btgp", c_im, h_im))
    return h_re[:, -1], h_im[:, -1], y


def s5_mixer(u, h0_re, h0_im, lw):
    f32 = jnp.float32
    bsz, t_len, _ = u.shape
    uf = u.astype(f32).reshape(bsz, t_len, N_SSM_GROUPS, SSM_GROUP)
    ab_re, ab_im, bb_re, bb_im = s5_discretize(lw["ssm_a_re"], lw["ssm_a_im"], lw["ssm_log_dt"],
                                               lw["ssm_b_re"], lw["ssm_b_im"])
    c_re = lw["ssm_c_re"].astype(f32)
    c_im = lw["ssm_c_im"].astype(f32)
    h0_re = h0_re.astype(f32)
    h0_im = h0_im.astype(f32)
    if t_len > CHUNK:
        n_blk = t_len // CHUNK
        ub = uf.reshape(bsz, n_blk, CHUNK, N_SSM_GROUPS, SSM_GROUP).transpose(1, 0, 2, 3, 4)

        def step(carry, u_blk):
            hr, hi = carry
            hr, hi, y_blk = s5_block(hr, hi, u_blk, ab_re, ab_im, bb_re, bb_im, c_re, c_im)
            return (hr, hi), y_blk

        (h_re, h_im), ys = lax.scan(step, (h0_re, h0_im), ub)
        y = ys.transpose(1, 0, 2, 3, 4).reshape(bsz, t_len, D_SSM)
    else:
        h_re, h_im, y = s5_block(h0_re, h0_im, uf, ab_re, ab_im, bb_re, bb_im, c_re, c_im)
        y = y.reshape(bsz, t_len, D_SSM)
    z = jax.nn.gelu(y + lw["ssm_d"].astype(f32) * u.astype(f32))
    out = z * jax.nn.sigmoid(z @ lw["glu_w"].astype(f32) + lw["glu_b"].astype(f32))
    return out.astype(u.dtype), h_re, h_im


def conv_mixer(pa, pg, conv_buf, lw):
    v = pa * jax.nn.sigmoid(pg)
    vp = jnp.concatenate([conv_buf.astype(v.dtype), v], axis=1)
    h = lax.conv_general_dilated(vp, lw["conv_w"][:, None, :].astype(v.dtype),
                                 window_strides=(1,), padding="VALID",
                                 dimension_numbers=("NWC", "WIO", "NWC"),
                                 feature_group_count=D_CONV)
    h = h + lw["conv_b"].astype(v.dtype)
    h = jax.nn.swish(layer_norm(h, lw["conv_ln_g"], lw["conv_ln_b"]))
    return h, vp[:, -(CONV_WIDTH - 1):]


def memory_kv(mem, w_k, w_v):
    bsz = mem.shape[0]
    k = (mem @ w_k).reshape(bsz, N_MEM, MEM_HEADS, MEM_HEAD_DIM)
    v = (mem @ w_v).reshape(bsz, N_MEM, MEM_HEADS, MEM_HEAD_DIM)
    return k, v


def memory_attend(x, mem_k, mem_v, w_q, w_o):
    bsz, t_len, _ = x.shape
    q = (x @ w_q).reshape(bsz, t_len, MEM_HEADS, MEM_HEAD_DIM)
    s = jnp.einsum("bthd,bmhd->bhtm", q, mem_k.astype(q.dtype),
                   preferred_element_type=jnp.float32) * (MEM_HEAD_DIM ** -0.5)
    p = jax.nn.softmax(s, axis=-1).astype(x.dtype)
    o = jnp.einsum("bhtm,bmhd->bthd", p, mem_v.astype(x.dtype)).reshape(bsz, t_len, D_MODEL)
    return o @ w_o


def encoder_layer(x, h0_re, h0_im, conv_buf, mem_k, mem_v, lw):
    proj = x @ lw["w_in"]
    u = proj[..., :D_SSM]
    pa = proj[..., D_SSM:D_SSM + D_CONV]
    pg = proj[..., D_SSM + D_CONV:]
    ya, h_re, h_im = s5_mixer(u, h0_re, h0_im, lw)
    yb, new_conv = conv_mixer(pa, pg, conv_buf, lw)
    mix = jnp.concatenate([ya, yb.astype(ya.dtype)], axis=-1) @ lw["w_out"]
    x = layer_norm(ALPHA * x + mix, lw["ln1_g"], lw["ln1_b"])
    att = memory_attend(x, mem_k, mem_v, lw["mem_w_q"], lw["mem_w_o"])
    x = layer_norm(ALPHA * x + att, lw["ln2_g"], lw["ln2_b"])
    hid = jnp.square(jax.nn.relu(x @ lw["mlp_w1"] + lw["mlp_b1"]))
    x = layer_norm(ALPHA * x + hid @ lw["mlp_w2"] + lw["mlp_b2"], lw["ln3_g"], lw["ln3_b"])
    return x, h_re, h_im, new_conv


def setup_inputs(seed: int = 0) -> dict:
    key = jax.random.key(seed)
    ks = jax.random.split(key, 40)
    f32 = jnp.float32
    nrm = lambda k, shape, s: (jax.random.normal(k, shape, f32) * s)
    L = DEPTH
    n_idx = jnp.arange(SSM_STATE, dtype=f32)
    inp = {}
    inp["x_prompt"] = nrm(ks[0], (BATCH, SEQ, D_MODEL), 1.0)
    inp["x_sample"] = nrm(ks[1], (DEC_BATCH, DEC_SEQ, D_MODEL), 1.0)
    inp["state_ssm_re"] = nrm(ks[2], (L, DEC_BATCH, N_SSM_GROUPS, SSM_STATE), 0.5)
    inp["state_ssm_im"] = nrm(ks[3], (L, DEC_BATCH, N_SSM_GROUPS, SSM_STATE), 0.5)
    inp["cache_conv"] = nrm(ks[4], (L, DEC_BATCH, CONV_WIDTH - 1, D_CONV), 1.0)
    inp["cache_mem_k"] = nrm(ks[5], (L, DEC_BATCH, N_MEM, MEM_HEADS, MEM_HEAD_DIM), 1.0)
    inp["cache_mem_v"] = nrm(ks[6], (L, DEC_BATCH, N_MEM, MEM_HEADS, MEM_HEAD_DIM), BETA)
    inp["mem_prompt"] = nrm(ks[7], (BATCH, N_MEM, D_MODEL), 1.0)
    inp["w_in"] = nrm(ks[8], (L, D_MODEL, D_IN), D_MODEL ** -0.5)
    inp["ssm_a_re"] = -0.5 + nrm(ks[9], (L, N_SSM_GROUPS, SSM_STATE), 0.01)
    inp["ssm_a_im"] = math.pi * n_idx + nrm(ks[10], (L, N_SSM_GROUPS, SSM_STATE), 0.01)
    inp["ssm_log_dt"] = jax.random.uniform(ks[11], (L, N_SSM_GROUPS), f32,
                                           minval=math.log(DT_MIN), maxval=math.log(DT_MAX))
    inp["ssm_b_re"] = nrm(ks[12], (L, N_SSM_GROUPS, SSM_STATE, SSM_GROUP), (2 * SSM_GROUP) ** -0.5)
    inp["ssm_b_im"] = nrm(ks[13], (L, N_SSM_GROUPS, SSM_STATE, SSM_GROUP), (2 * SSM_GROUP) ** -0.5)
    inp["ssm_c_re"] = nrm(ks[14], (L, N_SSM_GROUPS, SSM_GROUP, SSM_STATE), SSM_STATE ** -0.5)
    inp["ssm_c_im"] = nrm(ks[15], (L, N_SSM_GROUPS, SSM_GROUP, SSM_STATE), SSM_STATE ** -0.5)
    inp["ssm_d"] = nrm(ks[16], (L, D_SSM), 1.0)
    inp["glu_w"] = nrm(ks[17], (L, D_SSM, D_SSM), D_SSM ** -0.5)
    inp["glu_b"] = nrm(ks[18], (L, D_SSM), 0.01)
    inp["conv_w"] = nrm(ks[19], (L, CONV_WIDTH, D_CONV), CONV_WIDTH ** -0.5)
    inp["conv_b"] = nrm(ks[20], (L, D_CONV), 0.01)
    inp["conv_ln_g"] = 1.0 + nrm(ks[21], (L, D_CONV), 0.02)
    inp["conv_ln_b"] = nrm(ks[22], (L, D_CONV), 0.02)
    inp["w_out"] = nrm(ks[23], (L, D_MODEL, D_MODEL), BETA * D_MODEL ** -0.5)
    inp["ln1_g"] = 1.0 + nrm(ks[24], (L, D_MODEL), 0.02)
    inp["ln1_b"] = nrm(ks[25], (L, D_MODEL), 0.02)
    inp["mem_w_q"] = nrm(ks[26], (L, D_MODEL, D_MODEL), D_MODEL ** -0.5)
    inp["mem_w_k"] = nrm(ks[27], (L, D_MODEL, D_MODEL), D_MODEL ** -0.5)
    inp["mem_w_v"] = nrm(ks[28], (L, D_MODEL, D_MODEL), BETA * D_MODEL ** -0.5)
    inp["mem_w_o"] = nrm(ks[29], (L, D_MODEL, D_MODEL), BETA * D_MODEL ** -0.5)
    inp["ln2_g"] = 1.0 + nrm(ks[30], (L, D_MODEL), 0.02)
    inp["ln2_b"] = nrm(ks[31], (L, D_MODEL), 0.02)
    inp["mlp_w1"] = nrm(ks[32], (L, D_MODEL, D_FF), BETA * D_MODEL ** -0.5)
    inp["mlp_b1"] = nrm(ks[33], (L, D_FF), 0.01)
    inp["mlp_w2"] = nrm(ks[34], (L, D_FF, D_MODEL), BETA * D_FF ** -0.5)
    inp["mlp_b2"] = nrm(ks[35], (L, D_MODEL), 0.01)
    inp["ln3_g"] = 1.0 + nrm(ks[36], (L, D_MODEL), 0.02)
    inp["ln3_b"] = nrm(ks[37], (L, D_MODEL), 0.02)
    return inp


def reference(x_prompt, x_sample, state_ssm_re, state_ssm_im, cache_conv, cache_mem_k, cache_mem_v,
              mem_prompt, w_in, ssm_a_re, ssm_a_im, ssm_log_dt, ssm_b_re, ssm_b_im, ssm_c_re, ssm_c_im,
              ssm_d, glu_w, glu_b, conv_w, conv_b, conv_ln_g, conv_ln_b, w_out, ln1_g, ln1_b,
              mem_w_q, mem_w_k, mem_w_v, mem_w_o, ln2_g, ln2_b,
              mlp_w1, mlp_b1, mlp_w2, mlp_b2, ln3_g, ln3_b):
    yp = x_prompt
    ys = x_sample
    p_re, p_im, p_conv, p_mk, p_mv = [], [], [], [], []
    s_re, s_im, s_conv = [], [], []
    zero_h = jnp.zeros((x_prompt.shape[0], N_SSM_GROUPS, SSM_STATE), jnp.float32)
    zero_conv = jnp.zeros((x_prompt.shape[0], CONV_WIDTH - 1, D_CONV), x_prompt.dtype)
    for l in range(DEPTH):
        lw = dict(w_in=w_in[l], ssm_a_re=ssm_a_re[l], ssm_a_im=ssm_a_im[l], ssm_log_dt=ssm_log_dt[l],
                  ssm_b_re=ssm_b_re[l], ssm_b_im=ssm_b_im[l], ssm_c_re=ssm_c_re[l], ssm_c_im=ssm_c_im[l],
                  ssm_d=ssm_d[l], glu_w=glu_w[l], glu_b=glu_b[l], conv_w=conv_w[l], conv_b=conv_b[l],
                  conv_ln_g=conv_ln_g[l], conv_ln_b=conv_ln_b[l], w_out=w_out[l],
                  ln1_g=ln1_g[l], ln1_b=ln1_b[l], mem_w_q=mem_w_q[l], mem_w_o=mem_w_o[l],
                  ln2_g=ln2_g[l], ln2_b=ln2_b[l], mlp_w1=mlp_w1[l], mlp_b1=mlp_b1[l],
                  mlp_w2=mlp_w2[l], mlp_b2=mlp_b2[l], ln3_g=ln3_g[l], ln3_b=ln3_b[l])
        mk, mv = memory_kv(mem_prompt, mem_w_k[l], mem_w_v[l])
        yp, hr, hi, cb = encoder_layer(yp, zero_h, zero_h, zero_conv, mk, mv, lw)
        p_re.append(hr)
        p_im.append(hi)
        p_conv.append(cb)
        p_mk.append(mk)
        p_mv.append(mv)
        ys, hr, hi, cb = encoder_layer(ys, state_ssm_re[l], state_ssm_im[l], cache_conv[l],
                                       cache_mem_k[l], cache_mem_v[l], lw)
        s_re.append(hr)
        s_im.append(hi)
        s_conv.append(cb)
    return (yp, ys, jnp.stack(p_re), jnp.stack(p_im), jnp.stack(p_conv), jnp.stack(p_mk),
            jnp.stack(p_mv), jnp.stack(s_re), jnp.stack(s_im), jnp.stack(s_conv))
```

```python
import functools
import math

import jax
import jax.numpy as jnp
from jax import lax
from jax.experimental import pallas as pl
from jax.experimental.pallas import tpu as pltpu

F32 = jnp.float32
BF16 = jnp.bfloat16

D_MODEL = 1024
D_SSM = 512
D_CONV = 512
SSM_GROUP = 16
N_GROUPS = 32
SSM_STATE = 64
N_STATE = N_GROUPS * SSM_STATE
CONV_WIDTH = 31
HALO = CONV_WIDTH - 1
N_MEM = 256
MEM_HEADS = 4
HEAD_DIM = 256
D_FF = 4096
D_IN = D_SSM + 2 * D_CONV
LN_EPS = 1e-5
DEPTH = 1
ALPHA = (2.0 * DEPTH) ** 0.25

LANE = 128
SUBLANE = 8
N_TILES = N_STATE // LANE
X_LB = D_MODEL // LANE
C_LB = D_CONV // LANE
VMEM_LIMIT = 56 * 1024 * 1024

PROMPT_S = 8
PROMPT_M = 64
ROWS_B = 512


def _layer_norm(x, g, b):
    mu = jnp.mean(x, axis=-1, keepdims=True)
    xc = x - mu
    var = jnp.mean(xc * xc, axis=-1, keepdims=True)
    return xc * lax.rsqrt(var + LN_EPS) * g + b


def _gelu_tanh(x):
    return 0.5 * x * (1.0 + jnp.tanh(math.sqrt(2.0 / math.pi) * (x + 0.044715 * (x * x * x))))


def _row_loop(n_rows, chunk, fn):
    def body(i, c):
        fn(pl.multiple_of(i * chunk, chunk))
        return c
    lax.fori_loop(0, n_rows // chunk, body, 0)


def _bdot(a, b):
    return jnp.dot(a, b, preferred_element_type=F32)


def _kv_kernel(mem_ref, wk_ref, wv_ref, k_ref, v_ref):
    mb = mem_ref[...].astype(BF16)
    k_ref[...] = _bdot(mb, wk_ref[...])
    v_ref[...] = _bdot(mb, wv_ref[...])


def _memory_kv(mem, wk, wv):
    nb = mem.shape[0]
    wspec = pl.BlockSpec((D_MODEL, D_MODEL), lambda b: (0, 0), pipeline_mode=pl.Buffered(1))
    ospec = pl.BlockSpec((None, N_MEM, D_MODEL), lambda b: (b, 0, 0))
    return pl.pallas_call(
        _kv_kernel,
        grid=(nb,),
        in_specs=[pl.BlockSpec((None, N_MEM, D_MODEL), lambda b: (b, 0, 0)), wspec, wspec],
        out_specs=[ospec, ospec],
        out_shape=[jax.ShapeDtypeStruct((nb, N_MEM, D_MODEL), F32)] * 2,
        compiler_params=pltpu.CompilerParams(dimension_semantics=("arbitrary",),
                                             vmem_limit_bytes=VMEM_LIMIT),
        name="memory_kv",
    )(mem, wk, wv)


def _mixer_kernel(x_ref, h0r_ref, h0i_ref, cache_ref, w_in_ref, wb_ref, cw_ref, pwr_ref, pwi_ref,
                  d_ref, gluw_ref, glub_ref, convw_ref, convb_ref, clng_ref, clnb_ref,
                  wout_ref, ln1g_ref, ln1b_ref,
                  x1_ref, hr_out, hi_out, conv_out,
                  xb, proj, ub, bu, hb, vbuf, tail, mixb, ire, iim, car_r, car_i,
                  *, S, m, chain):
    TB = S * m
    i = pl.program_id(1)
    last = pl.num_programs(1) - 1
    n_sub = S // SUBLANE

    def load_step(ref, j):
        return jnp.concatenate(
            [ref[pl.ds(j * X_LB + lb, S, stride=m * X_LB), :] for lb in range(X_LB)], axis=1)

    def perm_body(jp, c):
        j0 = jp * 2
        xb[pl.ds(pl.multiple_of(jp * 2 * S, 2 * S), 2 * S), :] = (
            jnp.concatenate([load_step(x_ref, j0), load_step(x_ref, j0 + 1)], axis=0).astype(BF16))
        return c
    lax.fori_loop(0, m // 2, perm_body, 0)
    proj[...] = _bdot(xb[...], w_in_ref[...])

    if chain:
        @pl.when(i == 0)
        def _():
            for k in range(HALO):
                tail[k * S:(k + 1) * S, :] = jnp.broadcast_to(cache_ref[k:k + 1, :], (S, D_CONV))
            ire[0:1, :] = h0r_ref[...]
            iim[0:1, :] = h0i_ref[...]

        @pl.when(i > 0)
        def _():
            tail[...] = vbuf[m * S:(m + HALO) * S, :]
            ire[0:1, :] = car_r[...]
            iim[0:1, :] = car_i[...]

    def uv_rows(r0):
        rows = pl.ds(r0, 64)
        ub[rows, :] = proj[rows, 0:D_SSM].astype(BF16)
        vbuf[pl.ds(HALO * S + r0, 64), :] = (
            proj[rows, D_SSM:D_SSM + D_CONV] * jax.nn.sigmoid(proj[rows, D_SSM + D_CONV:D_IN]))
    _row_loop(TB, 64, uv_rows)

    if chain:
        sub = lax.broadcasted_iota(jnp.int32, (S, D_CONV), 0)
        for k in range(HALO):
            cur = vbuf[(m + k) * S:(m + k + 1) * S, :]
            prev = tail[k * S:(k + 1) * S, :]
            vbuf[k * S:(k + 1) * S, :] = pltpu.roll(jnp.where(sub == S - 1, prev, cur), 1, 0)
    else:
        for k in range(HALO):
            vbuf[k * S:(k + 1) * S, :] = jnp.concatenate(
                [cache_ref[pl.ds(k * C_LB + lb, S, stride=HALO * C_LB), :] for lb in range(C_LB)], axis=1)

    for c in range(N_TILES):
        kb = c // 4
        res = _bdot(ub[:, kb * LANE:(kb + 1) * LANE], wb_ref[c])
        bu[:, c * LANE:(c + 1) * LANE] = res[:, 0:LANE]
        bu[:, N_STATE + c * LANE:N_STATE + (c + 1) * LANE] = res[:, LANE:2 * LANE]

    lch = (8 * 1024) // S
    for lc in range(N_STATE // lch):
        l0 = lc * lch
        ar = jnp.broadcast_to(pwr_ref[0:1, l0:l0 + lch], (S, lch))
        ai = jnp.broadcast_to(pwi_ref[0:1, l0:l0 + lch], (S, lch))

        def scan_body(j, carry, l0=l0, ar=ar, ai=ai):
            hr, hi = carry
            rows = pl.ds(pl.multiple_of(j * S, S), S)
            nr = ar * hr - ai * hi + bu[rows, l0:l0 + lch]
            ni = ar * hi + ai * hr + bu[rows, N_STATE + l0:N_STATE + l0 + lch]
            bu[rows, l0:l0 + lch] = nr
            bu[rows, N_STATE + l0:N_STATE + l0 + lch] = ni
            return nr, ni
        z = jnp.zeros((S, lch), F32)
        lax.fori_loop(0, m, scan_body, (z, z), unroll=2)

    if chain:
        amr = pwr_ref[m - 1:m, :]
        ami = pwi_ref[m - 1:m, :]
        cr = ire[0:1, :]
        ci = iim[0:1, :]
        for s in range(S):
            row = (m - 1) * S + s
            zr = bu[row:row + 1, 0:N_STATE]
            zi = bu[row:row + 1, N_STATE:2 * N_STATE]
            cr, ci = amr * cr - ami * ci + zr, amr * ci + ami * cr + zi
            if s < S - 1:
                ire[s + 1:s + 2, :] = cr
                iim[s + 1:s + 2, :] = ci
        car_r[...] = cr
        car_i[...] = ci

        @pl.when(i == last)
        def _():
            hr_out[...] = cr
            hi_out[...] = ci
    in_r = ire if chain else h0r_ref
    in_i = iim if chain else h0i_ref

    jstep = max(1, 16 // S)
    for lc in range(N_STATE // 512):
        l0 = lc * 512
        e_r = in_r[:, l0:l0 + 512]
        e_i = in_i[:, l0:l0 + 512]

        def fix_body(jp, c, l0=l0, e_r=e_r, e_i=e_i):
            hs_r, hs_i = [], []
            for q in range(jstep):
                j = jp * jstep + q
                rows = pl.ds(pl.multiple_of(j * S, S), S)
                pr = pwr_ref[pl.ds(j, 1), l0:l0 + 512]
                pi = pwi_ref[pl.ds(j, 1), l0:l0 + 512]
                hs_r.append(bu[rows, l0:l0 + 512] + pr * e_r - pi * e_i)
                hs_i.append(bu[rows, N_STATE + l0:N_STATE + l0 + 512] + pr * e_i + pi * e_r)
            orow = pl.ds(pl.multiple_of(jp * jstep * S, jstep * S), jstep * S)
            hb[orow, l0:l0 + 512] = jnp.concatenate(hs_r, axis=0).astype(BF16)
            hb[orow, N_STATE + l0:N_STATE + l0 + 512] = jnp.concatenate(hs_i, axis=0).astype(BF16)
            return c
        lax.fori_loop(0, m // jstep, fix_body, 0)

    if not chain:
        rows = slice((m - 1) * S, m * S)
        pr = pwr_ref[m - 1:m, :]
        pi = pwi_ref[m - 1:m, :]
        e_r = in_r[...]
        e_i = in_i[...]
        hr_out[...] = bu[rows, 0:N_STATE] + pr * e_r - pi * e_i
        hi_out[...] = bu[rows, N_STATE:2 * N_STATE] + pr * e_i + pi * e_r

    y0 = D_SSM + D_CONV
    for ob in range(D_SSM // LANE):
        acc = None
        for c in range(4 * ob, 4 * ob + 4):
            lhs = jnp.concatenate([hb[:, c * LANE:(c + 1) * LANE],
                                   hb[:, N_STATE + c * LANE:N_STATE + (c + 1) * LANE]], axis=1)
            t = _bdot(lhs, cw_ref[c])
            acc = t if acc is None else acc + t
        proj[:, y0 + ob * LANE:y0 + (ob + 1) * LANE] = acc

    def gelu_rows(r0):
        rows = pl.ds(r0, 64)
        z = _gelu_tanh(proj[rows, y0:y0 + D_SSM] + d_ref[...] * proj[rows, 0:D_SSM])
        proj[rows, y0:y0 + D_SSM] = z
        mixb[rows, 0:D_SSM] = z.astype(BF16)
    _row_loop(TB, 64, gelu_rows)
    proj[:, D_SSM:D_SSM + D_CONV] = _bdot(mixb[:, 0:D_SSM], gluw_ref[...])

    def glu_rows(r0):
        rows = pl.ds(r0, 64)
        gate = jax.nn.sigmoid(proj[rows, D_SSM:D_SSM + D_CONV] + glub_ref[...])
        mixb[rows, 0:D_SSM] = (proj[rows, y0:y0 + D_SSM] * gate).astype(BF16)
    _row_loop(TB, 64, glu_rows)

    jb = 4
    for lb in range(D_CONV // LANE):
        lanes = slice(lb * LANE, (lb + 1) * LANE)
        wv = [jnp.broadcast_to(convw_ref[k:k + 1, lanes], (SUBLANE, LANE)) for k in range(CONV_WIDTH)]
        bias = jnp.broadcast_to(convb_ref[:, lanes], (SUBLANE, LANE))
        for sg in range(n_sub):
            def conv_body(jg, c, lanes=lanes, wv=wv, bias=bias, sg=sg):
                j0 = jg * jb
                accs = [bias] * jb
                for t in range(jb + HALO):
                    row = vbuf[pl.ds(pl.multiple_of((j0 + t) * S + sg * SUBLANE, SUBLANE), SUBLANE), lanes]
                    for q in range(jb):
                        k = t - q
                        if 0 <= k < CONV_WIDTH:
                            accs[q] = accs[q] + wv[k] * row
                for q in range(jb):
                    proj[pl.ds(pl.multiple_of((j0 + q) * S + sg * SUBLANE, SUBLANE), SUBLANE), lanes] = accs[q]
                return c
            lax.fori_loop(0, m // jb, conv_body, 0)

    def convln_rows(r0):
        rows = pl.ds(r0, 64)
        h = _layer_norm(proj[rows, 0:D_CONV], clng_ref[...], clnb_ref[...])
        mixb[rows, D_SSM:D_MODEL] = (h * jax.nn.sigmoid(h)).astype(BF16)
    _row_loop(TB, 64, convln_rows)

    if chain:
        @pl.when(i == last)
        def _():
            for k in range(HALO):
                row = (m + k) * S + S - 1
                conv_out[k:k + 1, :] = vbuf[row:row + 1, :]
    else:
        for k in range(HALO):
            for lb in range(C_LB):
                conv_out[pl.ds(k * C_LB + lb, S, stride=HALO * C_LB), :] = (
                    vbuf[(m + k) * S:(m + k + 1) * S, lb * LANE:(lb + 1) * LANE])

    proj[:, 0:D_MODEL] = _bdot(mixb[...], wout_ref[...])

    def out_body(j, c):
        rows = pl.ds(pl.multiple_of(j * S, S), S)
        y = _layer_norm(ALPHA * load_step(x_ref, j) + proj[rows, 0:D_MODEL], ln1g_ref[...], ln1b_ref[...])
        for lb in range(X_LB):
            x1_ref[pl.ds(j * X_LB + lb, S, stride=m * X_LB), :] = y[:, lb * LANE:(lb + 1) * LANE]
        return c
    lax.fori_loop(0, m, out_body, 0, unroll=2)


def _const_spec(shape):
    nd = len(shape)
    return pl.BlockSpec(shape, lambda b, i: (0,) * nd, pipeline_mode=pl.Buffered(1))


def _mixer(x, h0r, h0i, cache, wts, *, S, m, chain):
    nb, T, _ = x.shape
    TB = S * m
    nblk = T // TB
    assert T == nblk * TB and S % SUBLANE == 0 and m % 4 == 0
    assert (not chain) or (S == SUBLANE and m >= HALO)
    s0 = h0r.shape[1]
    cshape = (HALO, D_CONV) if chain else (S * HALO * C_LB, LANE)
    bspec = lambda shape: pl.BlockSpec((None,) + shape, lambda b, i: (b, 0, 0))
    xspec = pl.BlockSpec((None, TB * X_LB, LANE), lambda b, i: (b, i, 0))
    in_specs = [xspec, bspec((s0, N_STATE)), bspec((s0, N_STATE)), bspec(cshape)]
    in_specs += [_const_spec(w.shape) for w in wts]
    out_specs = [xspec, bspec((s0, N_STATE)), bspec((s0, N_STATE)), bspec(cshape)]
    out_shape = [jax.ShapeDtypeStruct((nb, T * X_LB, LANE), F32),
                 jax.ShapeDtypeStruct((nb, s0, N_STATE), F32),
                 jax.ShapeDtypeStruct((nb, s0, N_STATE), F32),
                 jax.ShapeDtypeStruct((nb,) + cshape, F32)]
    x = x.reshape(nb, T * X_LB, LANE)
    cache = cache.reshape((nb,) + cshape)
    scratch = [pltpu.VMEM((TB, D_MODEL), BF16),
               pltpu.VMEM((TB, D_IN), F32),
               pltpu.VMEM((TB, D_SSM), BF16),
               pltpu.VMEM((TB, 2 * N_STATE), F32),
               pltpu.VMEM((TB, 2 * N_STATE), BF16),
               pltpu.VMEM(((m + HALO) * S, D_CONV), F32),
               pltpu.VMEM((HALO * S, D_CONV), F32),
               pltpu.VMEM((TB, D_MODEL), BF16),
               pltpu.VMEM((S, N_STATE), F32),
               pltpu.VMEM((S, N_STATE), F32),
               pltpu.VMEM((1, N_STATE), F32),
               pltpu.VMEM((1, N_STATE), F32)]
    return pl.pallas_call(
        functools.partial(_mixer_kernel, S=S, m=m, chain=chain),
        grid=(nb, nblk),
        in_specs=in_specs,
        out_specs=out_specs,
        out_shape=out_shape,
        scratch_shapes=scratch,
        compiler_params=pltpu.CompilerParams(dimension_semantics=("arbitrary", "arbitrary"),
                                             vmem_limit_bytes=VMEM_LIMIT),
        name="mixer_chain" if chain else "mixer_streams",
    )(x, h0r, h0i, cache, *wts)


def _attend_rows(qb, kb, vb, sbuf, pbuf, ob, q_rows, n_rows):
    chunk = min(n_rows, 64)
    for h in range(MEM_HEADS):
        cols = slice(h * HEAD_DIM, (h + 1) * HEAD_DIM)
        sbuf[...] = lax.dot_general(qb[q_rows, cols], kb[:, cols], (((1,), (1,)), ((), ())),
                                    preferred_element_type=F32)

        def softmax_rows(r0):
            rows = pl.ds(r0, chunk)
            s = sbuf[rows, :] * (HEAD_DIM ** -0.5)
            e = jnp.exp(s - jnp.max(s, axis=-1, keepdims=True))
            pbuf[rows, :] = (e / jnp.sum(e, axis=-1, keepdims=True)).astype(BF16)
        _row_loop(n_rows, chunk, softmax_rows)
        ob[q_rows, cols] = _bdot(pbuf[...], vb[:, cols]).astype(BF16)


def _post_attention(n_rows, x1_ref, ob, wo_ref, ln2g, ln2b, w1_ref, b1_ref, w2_ref, b2_ref, ln3g, ln3b,
                    out_ref, fbuf, x2buf, xb, hidb):
    fbuf[...] = _bdot(ob[...], wo_ref[...])

    def ln2_rows(r0):
        rows = pl.ds(r0, 32)
        x2 = _layer_norm(ALPHA * x1_ref[rows, :] + fbuf[rows, :], ln2g[...], ln2b[...])
        x2buf[rows, :] = x2
        xb[rows, :] = x2.astype(BF16)
    _row_loop(n_rows, 32, ln2_rows)

    for n in range(D_FF // D_MODEL):
        cols = slice(n * D_MODEL, (n + 1) * D_MODEL)
        fbuf[...] = _bdot(xb[...], w1_ref[:, cols])

        def relu2_rows(r0, cols=cols):
            rows = pl.ds(r0, 32)
            a = jnp.maximum(fbuf[rows, :] + b1_ref[:, cols], 0.0)
            hidb[rows, cols] = (a * a).astype(BF16)
        _row_loop(n_rows, 32, relu2_rows)
    fbuf[...] = _bdot(hidb[...], w2_ref[...])

    def ln3_rows(r0):
        rows = pl.ds(r0, 32)
        out_ref[rows, :] = _layer_norm(ALPHA * x2buf[rows, :] + fbuf[rows, :] + b2_ref[...],
                                       ln3g[...], ln3b[...])
    _row_loop(n_rows, 32, ln3_rows)


def _attn_mlp_prompt_kernel(x1_ref, k_ref, v_ref, wq_ref, wo_ref, ln2g, ln2b, w1_ref, b1_ref,
                            w2_ref, b2_ref, ln3g, ln3b, out_ref,
                            xb, qb, ob, kb, vb, sbuf, pbuf, fbuf, x2buf, hidb):
    n_rows = x1_ref.shape[0]
    kb[...] = k_ref[...].astype(BF16)
    vb[...] = v_ref[...].astype(BF16)

    def cast_rows(r0):
        rows = pl.ds(r0, 64)
        xb[rows, :] = x1_ref[rows, :].astype(BF16)
    _row_loop(n_rows, 64, cast_rows)
    fbuf[...] = _bdot(xb[...], wq_ref[...])

    def q_rows(r0):
        rows = pl.ds(r0, 64)
        qb[rows, :] = fbuf[rows, :].astype(BF16)
    _row_loop(n_rows, 64, q_rows)
    _attend_rows(qb, kb, vb, sbuf, pbuf, ob, slice(None), n_rows)
    _post_attention(n_rows, x1_ref, ob, wo_ref, ln2g, ln2b, w1_ref, b1_ref, w2_ref, b2_ref, ln3g, ln3b,
                    out_ref, fbuf, x2buf, xb, hidb)


def _attn_mlp_streams_kernel(x1_ref, k_ref, v_ref, wq_ref, wo_ref, ln2g, ln2b, w1_ref, b1_ref,
                             w2_ref, b2_ref, ln3g, ln3b, out_ref,
                             xb, qb, ob, kb, vb, sbuf, pbuf, fbuf, x2buf, hidb, *, t_len):
    n_rows = x1_ref.shape[0]
    s_id = pl.program_id(0)

    @pl.when(s_id == 0)
    def _():
        xb[...] = x1_ref[...].astype(BF16)
        qb[...] = _bdot(xb[...], wq_ref[...]).astype(BF16)

    kb[...] = k_ref[...].astype(BF16)
    vb[...] = v_ref[...].astype(BF16)
    _attend_rows(qb, kb, vb, sbuf, pbuf, ob, pl.ds(pl.multiple_of(s_id * t_len, t_len), t_len), t_len)

    @pl.when(s_id == pl.num_programs(0) - 1)
    def _():
        _post_attention(n_rows, x1_ref, ob, wo_ref, ln2g, ln2b, w1_ref, b1_ref, w2_ref, b2_ref,
                        ln3g, ln3b, out_ref, fbuf, x2buf, xb, hidb)


def _attn_mlp_scratch(rows, srows):
    return [pltpu.VMEM((rows, D_MODEL), BF16),
            pltpu.VMEM((rows, D_MODEL), BF16),
            pltpu.VMEM((rows, D_MODEL), BF16),
            pltpu.VMEM((N_MEM, D_MODEL), BF16),
            pltpu.VMEM((N_MEM, D_MODEL), BF16),
            pltpu.VMEM((srows, N_MEM), F32),
            pltpu.VMEM((srows, N_MEM), BF16),
            pltpu.VMEM((rows, D_MODEL), F32),
            pltpu.VMEM((rows, D_MODEL), F32),
            pltpu.VMEM((rows, D_FF), BF16)]


def _w1(shape):
    nd = len(shape)
    return pl.BlockSpec(shape, lambda i: (0,) * nd, pipeline_mode=pl.Buffered(1))


def _attn_mlp_prompt(x1, mem_k, mem_v, wts, rows_per_stream):
    n = x1.shape[0]
    blk_per_stream = rows_per_stream // ROWS_B
    assert rows_per_stream % ROWS_B == 0
    kvspec = pl.BlockSpec((None, N_MEM, D_MODEL), lambda i: (i // blk_per_stream, 0, 0))
    xspec = pl.BlockSpec((ROWS_B, D_MODEL), lambda i: (i, 0))
    return pl.pallas_call(
        _attn_mlp_prompt_kernel,
        grid=(n // ROWS_B,),
        in_specs=[xspec, kvspec, kvspec] + [_w1(w.shape) for w in wts],
        out_specs=xspec,
        out_shape=jax.ShapeDtypeStruct((n, D_MODEL), F32),
        scratch_shapes=_attn_mlp_scratch(ROWS_B, ROWS_B),
        compiler_params=pltpu.CompilerParams(dimension_semantics=("arbitrary",),
                                             vmem_limit_bytes=VMEM_LIMIT),
        name="attn_mlp_prompt",
    )(x1, mem_k, mem_v, *wts)


def _attn_mlp_streams(x1, mem_k, mem_v, wts, t_len):
    n = x1.shape[0]
    n_streams = n // t_len
    kvspec = pl.BlockSpec((None, N_MEM, D_MODEL), lambda s: (s, 0, 0))
    xspec = pl.BlockSpec((n, D_MODEL), lambda s: (0, 0))
    return pl.pallas_call(
        functools.partial(_attn_mlp_streams_kernel, t_len=t_len),
        grid=(n_streams,),
        in_specs=[xspec, kvspec, kvspec] + [_w1(w.shape) for w in wts],
        out_specs=xspec,
        out_shape=jax.ShapeDtypeStruct((n, D_MODEL), F32),
        scratch_shapes=_attn_mlp_scratch(n, t_len),
        compiler_params=pltpu.CompilerParams(dimension_semantics=("arbitrary",),
                                             vmem_limit_bytes=VMEM_LIMIT),
        name="attn_mlp_streams",
    )(x1, mem_k, mem_v, *wts)


def _s5_tables(a_re, a_im, log_dt, b_re, b_im, c_re, c_im, n_steps):
    dt = jnp.exp(log_dt)[:, None]
    mag = jnp.exp(a_re * dt)
    ang = a_im * dt
    ab_re = mag * jnp.cos(ang)
    ab_im = mag * jnp.sin(ang)
    den = a_re * a_re + a_im * a_im
    p = ab_re - 1.0
    q = ab_im
    k_re = ((p * a_re + q * a_im) / den)[..., None]
    k_im = ((q * a_re - p * a_im) / den)[..., None]
    bb_re = k_re * b_re - k_im * b_im
    bb_im = k_re * b_im + k_im * b_re

    steps = jnp.arange(1, n_steps + 1, dtype=F32)[:, None, None]
    pmag = jnp.exp(a_re * dt * steps)
    pang = a_im * dt * steps
    pw_re = (pmag * jnp.cos(pang)).reshape(n_steps, N_STATE)
    pw_im = (pmag * jnp.sin(pang)).reshape(n_steps, N_STATE)
    pw_re = pw_re.at[0].set(ab_re.reshape(N_STATE))
    pw_im = pw_im.at[0].set(ab_im.reshape(N_STATE))

    eye = jnp.eye(N_GROUPS, dtype=F32)
    full = lambda b: jnp.einsum("gnp,gh->gphn", b, eye).reshape(D_SSM, N_STATE)
    fb_re, fb_im = full(bb_re), full(bb_im)
    wb = jnp.stack([
        jnp.concatenate([fb_re[(c // 4) * LANE:(c // 4 + 1) * LANE, c * LANE:(c + 1) * LANE],
                         fb_im[(c // 4) * LANE:(c // 4 + 1) * LANE, c * LANE:(c + 1) * LANE]], axis=1)
        for c in range(N_TILES)]).astype(BF16)
    fullc = lambda cm: jnp.einsum("gpn,gh->hngp", cm, eye).reshape(N_STATE, D_SSM)
    fc_re, fc_im = fullc(c_re), fullc(c_im)
    cw = jnp.stack([
        jnp.concatenate([fc_re[c * LANE:(c + 1) * LANE, (c // 4) * LANE:(c // 4 + 1) * LANE],
                         -fc_im[c * LANE:(c + 1) * LANE, (c // 4) * LANE:(c // 4 + 1) * LANE]], axis=0)
        for c in range(N_TILES)]).astype(BF16)
    return wb, cw, pw_re, pw_im


def kernel(x_prompt, x_sample, state_ssm_re, state_ssm_im, cache_conv, cache_mem_k, cache_mem_v, mem_prompt, w_in, ssm_a_re, ssm_a_im, ssm_log_dt, ssm_b_re, ssm_b_im, ssm_c_re, ssm_c_im, ssm_d, glu_w, glu_b, conv_w, conv_b, conv_ln_g, conv_ln_b, w_out, ln1_g, ln1_b, mem_w_q, mem_w_k, mem_w_v, mem_w_o, ln2_g, ln2_b, mlp_w1, mlp_b1, mlp_w2, mlp_b2, ln3_g, ln3_b):
    assert w_in.shape[0] == DEPTH == 1
    nb, t_prompt, _ = x_prompt.shape
    n_dec, t_dec, _ = x_sample.shape
    row = lambda v: v[0].reshape(1, -1)

    def mixer_weights(n_steps):
        wb, cw, pw_re, pw_im = _s5_tables(ssm_a_re[0], ssm_a_im[0], ssm_log_dt[0], ssm_b_re[0],
                                          ssm_b_im[0], ssm_c_re[0], ssm_c_im[0], n_steps)
        return (w_in[0].astype(BF16), wb, cw, pw_re, pw_im, row(ssm_d), glu_w[0].astype(BF16),
                row(glu_b), conv_w[0], row(conv_b), row(conv_ln_g), row(conv_ln_b),
                w_out[0].astype(BF16), row(ln1_g), row(ln1_b))

    post_weights = (mem_w_q[0].astype(BF16), mem_w_o[0].astype(BF16), row(ln2_g), row(ln2_b),
                    mlp_w1[0].astype(BF16), row(mlp_b1), mlp_w2[0].astype(BF16), row(mlp_b2),
                    row(ln3_g), row(ln3_b))

    mk, mv = _memory_kv(mem_prompt, mem_w_k[0].astype(BF16), mem_w_v[0].astype(BF16))
    zero_h = jnp.zeros((nb, 1, N_STATE), F32)
    zero_conv = jnp.zeros((nb, HALO, D_CONV), x_prompt.dtype)
    x1p, p_re, p_im, p_conv = _mixer(x_prompt, zero_h, zero_h, zero_conv, mixer_weights(PROMPT_M),
                                     S=PROMPT_S, m=PROMPT_M, chain=True)
    yp = _attn_mlp_prompt(x1p.reshape(nb * t_prompt, D_MODEL), mk, mv, post_weights, t_prompt)
    yp = yp.reshape(nb, t_prompt, D_MODEL)

    x1s, s_re, s_im, s_conv = _mixer(
        x_sample.reshape(1, n_dec * t_dec, D_MODEL),
        state_ssm_re[0].reshape(1, n_dec, N_STATE), state_ssm_im[0].reshape(1, n_dec, N_STATE),
        cache_conv[0].reshape(1, n_dec * HALO, D_CONV), mixer_weights(t_dec),
        S=n_dec, m=t_dec, chain=False)
    ys = _attn_mlp_streams(x1s.reshape(n_dec * t_dec, D_MODEL),
                           cache_mem_k[0].reshape(n_dec, N_MEM, D_MODEL),
                           cache_mem_v[0].reshape(n_dec, N_MEM, D_MODEL), post_weights, t_dec)
    ys = ys.reshape(n_dec, t_dec, D_MODEL)

    st = lambda a, n: a.reshape(1, n, N_GROUPS, SSM_STATE)
    kv5 = lambda a: a.reshape(1, nb, N_MEM, MEM_HEADS, HEAD_DIM)
    return (yp, ys, st(p_re, nb), st(p_im, nb), p_conv.reshape(1, nb, HALO, D_CONV), kv5(mk), kv5(mv),
            st(s_re, n_dec), st(s_im, n_dec), s_conv.reshape(1, n_dec, HALO, D_CONV))
```

```python
import functools
import math

import jax
import jax.numpy as jnp
from jax import lax
from jax.experimental import pallas as pl
from jax.experimental.pallas import tpu as pltpu

F32 = jnp.float32
BF16 = jnp.bfloat16

D_MODEL = 1024
D_SSM = 512
D_CONV = 512
SSM_GROUP = 16
N_GROUPS = 32
SSM_STATE = 64
N_STATE = N_GROUPS * SSM_STATE
CONV_WIDTH = 31
HALO = CONV_WIDTH - 1
N_MEM = 256
MEM_HEADS = 4
HEAD_DIM = 256
D_FF = 4096
D_IN = D_SSM + 2 * D_CONV
LN_EPS = 1e-5
DEPTH = 1
ALPHA = (2.0 * DEPTH) ** 0.25

LANE = 128
SUBLANE = 8
N_TILES = N_STATE // LANE
VMEM_LIMIT = 56 * 1024 * 1024

PROMPT_S = 8
PROMPT_M = 64
ROWS_B = 512
LN_ROWS = 32


def _layer_norm(x, g, b):
    mu = jnp.mean(x, axis=-1, keepdims=True)
    xc = x - mu
    var = jnp.mean(xc * xc, axis=-1, keepdims=True)
    return xc * lax.rsqrt(var + LN_EPS) * g + b


def _gelu_tanh(x):
    return 0.5 * x * (1.0 + jnp.tanh(math.sqrt(2.0 / math.pi) * (x + 0.044715 * (x * x * x))))


def _pieces(n_rows, chunk):
    chunk = min(chunk, n_rows)
    return [slice(r0, r0 + chunk) for r0 in range(0, n_rows, chunk)]


def _bdot(a, b):
    return jnp.dot(a, b, preferred_element_type=F32)


def _kv_kernel(mem_ref, wk_ref, wv_ref, k_ref, v_ref):
    mb = mem_ref[...].astype(BF16)
    k_ref[...] = _bdot(mb, wk_ref[...])
    v_ref[...] = _bdot(mb, wv_ref[...])


def _memory_kv(mem, wk, wv):
    nb = mem.shape[0]
    wspec = pl.BlockSpec((D_MODEL, D_MODEL), lambda b: (0, 0), pipeline_mode=pl.Buffered(1))
    ospec = pl.BlockSpec((None, N_MEM, D_MODEL), lambda b: (b, 0, 0))
    return pl.pallas_call(
        _kv_kernel,
        grid=(nb,),
        in_specs=[pl.BlockSpec((None, N_MEM, D_MODEL), lambda b: (b, 0, 0)), wspec, wspec],
        out_specs=[ospec, ospec],
        out_shape=[jax.ShapeDtypeStruct((nb, N_MEM, D_MODEL), F32)] * 2,
        compiler_params=pltpu.CompilerParams(dimension_semantics=("arbitrary",),
                                             vmem_limit_bytes=VMEM_LIMIT),
        name="memory_kv",
    )(mem, wk, wv)


def _mixer_kernel(x_ref, h0r_ref, h0i_ref, cache_ref, perm_ref, permt_ref, w_in_ref, wb_ref, cw_ref,
                  pwr_ref, pwi_ref, d_ref, gluw_ref, glub_ref, convw_ref, convb_ref, clng_ref, clnb_ref,
                  wout_ref, ln1g_ref, ln1b_ref,
                  x1_ref, hr_out, hi_out, conv_out,
                  xb, xp, proj, ub, bu, hb, vbuf, tail, mixb, ire, iim, car_r, car_i,
                  *, S, m, chain):
    TB = S * m
    i = pl.program_id(1)
    last = pl.num_programs(1) - 1
    n_sub = S // SUBLANE

    for rows in _pieces(TB, 64):
        xb[rows, :] = x_ref[rows, :].astype(BF16)
    xp[...] = _bdot(perm_ref[...], xb[...]).astype(BF16)
    proj[...] = _bdot(xp[...], w_in_ref[...])

    if chain:
        @pl.when(i == 0)
        def _():
            for k in range(HALO):
                tail[k * S:(k + 1) * S, :] = jnp.broadcast_to(cache_ref[k:k + 1, :], (S, D_CONV))
            ire[0:1, :] = h0r_ref[...]
            iim[0:1, :] = h0i_ref[...]

        @pl.when(i > 0)
        def _():
            tail[...] = vbuf[m * S:(m + HALO) * S, :]
            ire[0:1, :] = car_r[...]
            iim[0:1, :] = car_i[...]

    for rows in _pieces(TB, 64):
        ub[rows, :] = proj[rows, 0:D_SSM].astype(BF16)
        vbuf[HALO * S + rows.start:HALO * S + rows.stop, :] = (
            proj[rows, D_SSM:D_SSM + D_CONV] * jax.nn.sigmoid(proj[rows, D_SSM + D_CONV:D_IN]))

    if chain:
        sub = lax.broadcasted_iota(jnp.int32, (S, D_CONV), 0)
        for k in range(HALO):
            cur = vbuf[(m + k) * S:(m + k + 1) * S, :]
            prev = tail[k * S:(k + 1) * S, :]
            vbuf[k * S:(k + 1) * S, :] = pltpu.roll(jnp.where(sub == S - 1, prev, cur), 1, 0)
    else:
        vbuf[0:HALO * S, :] = cache_ref[...]

    for c in range(N_TILES):
        kb = c // 4
        res = _bdot(ub[:, kb * LANE:(kb + 1) * LANE], wb_ref[c])
        bu[:, c * LANE:(c + 1) * LANE] = res[:, 0:LANE]
        bu[:, N_STATE + c * LANE:N_STATE + (c + 1) * LANE] = res[:, LANE:2 * LANE]

    lch = (8 * 1024) // S
    for lc in range(N_STATE // lch):
        l0 = lc * lch
        ar = jnp.broadcast_to(pwr_ref[0:1, l0:l0 + lch], (S, lch))
        ai = jnp.broadcast_to(pwi_ref[0:1, l0:l0 + lch], (S, lch))

        def scan_body(j, carry, l0=l0, ar=ar, ai=ai):
            hr, hi = carry
            rows = pl.ds(pl.multiple_of(j * S, S), S)
            nr = ar * hr - ai * hi + bu[rows, l0:l0 + lch]
            ni = ar * hi + ai * hr + bu[rows, N_STATE + l0:N_STATE + l0 + lch]
            bu[rows, l0:l0 + lch] = nr
            bu[rows, N_STATE + l0:N_STATE + l0 + lch] = ni
            return nr, ni
        z = jnp.zeros((S, lch), F32)
        lax.fori_loop(0, m, scan_body, (z, z), unroll=2)

    if chain:
        amr = pwr_ref[m - 1:m, :]
        ami = pwi_ref[m - 1:m, :]
        cr = ire[0:1, :]
        ci = iim[0:1, :]
        for s in range(S):
            row = (m - 1) * S + s
            zr = bu[row:row + 1, 0:N_STATE]
            zi = bu[row:row + 1, N_STATE:2 * N_STATE]
            cr, ci = amr * cr - ami * ci + zr, amr * ci + ami * cr + zi
            if s < S - 1:
                ire[s + 1:s + 2, :] = cr
                iim[s + 1:s + 2, :] = ci
        car_r[...] = cr
        car_i[...] = ci

        @pl.when(i == last)
        def _():
            hr_out[...] = cr
            hi_out[...] = ci
    in_r = ire if chain else h0r_ref
    in_i = iim if chain else h0i_ref

    jstep = max(1, 16 // S)
    for lc in range(N_STATE // 512):
        l0 = lc * 512
        e_r = in_r[:, l0:l0 + 512]
        e_i = in_i[:, l0:l0 + 512]

        def fix_body(jp, c, l0=l0, e_r=e_r, e_i=e_i):
            hs_r, hs_i = [], []
            for q in range(jstep):
                j = jp * jstep + q
                rows = pl.ds(pl.multiple_of(j * S, S), S)
                pr = pwr_ref[pl.ds(j, 1), l0:l0 + 512]
                pi = pwi_ref[pl.ds(j, 1), l0:l0 + 512]
                hs_r.append(bu[rows, l0:l0 + 512] + pr * e_r - pi * e_i)
                hs_i.append(bu[rows, N_STATE + l0:N_STATE + l0 + 512] + pr * e_i + pi * e_r)
            orow = pl.ds(pl.multiple_of(jp * jstep * S, jstep * S), jstep * S)
            hb[orow, l0:l0 + 512] = jnp.concatenate(hs_r, axis=0).astype(BF16)
            hb[orow, N_STATE + l0:N_STATE + l0 + 512] = jnp.concatenate(hs_i, axis=0).astype(BF16)
            return c
        lax.fori_loop(0, m // jstep, fix_body, 0, unroll=2)

    if not chain:
        rows = slice((m - 1) * S, m * S)
        pr = pwr_ref[m - 1:m, :]
        pi = pwi_ref[m - 1:m, :]
        e_r = in_r[...]
        e_i = in_i[...]
        hr_out[...] = bu[rows, 0:N_STATE] + pr * e_r - pi * e_i
        hi_out[...] = bu[rows, N_STATE:2 * N_STATE] + pr * e_i + pi * e_r

    y0 = D_SSM + D_CONV
    for ob in range(D_SSM // LANE):
        acc = None
        for c in range(4 * ob, 4 * ob + 4):
            lhs = jnp.concatenate([hb[:, c * LANE:(c + 1) * LANE],
                                   hb[:, N_STATE + c * LANE:N_STATE + (c + 1) * LANE]], axis=1)
            t = _bdot(lhs, cw_ref[c])
            acc = t if acc is None else acc + t
        proj[:, y0 + ob * LANE:y0 + (ob + 1) * LANE] = acc

    for rows in _pieces(TB, 64):
        z = _gelu_tanh(proj[rows, y0:y0 + D_SSM] + d_ref[...] * proj[rows, 0:D_SSM])
        proj[rows, y0:y0 + D_SSM] = z
        mixb[rows, 0:D_SSM] = z.astype(BF16)
    proj[:, D_SSM:D_SSM + D_CONV] = _bdot(mixb[:, 0:D_SSM], gluw_ref[...])
    for rows in _pieces(TB, 64):
        gate = jax.nn.sigmoid(proj[rows, D_SSM:D_SSM + D_CONV] + glub_ref[...])
        mixb[rows, 0:D_SSM] = (proj[rows, y0:y0 + D_SSM] * gate).astype(BF16)

    jb = 8
    for lb in range(D_CONV // LANE):
        lanes = slice(lb * LANE, (lb + 1) * LANE)
        wv = [jnp.broadcast_to(convw_ref[k:k + 1, lanes], (SUBLANE, LANE)) for k in range(CONV_WIDTH)]
        bias = jnp.broadcast_to(convb_ref[:, lanes], (SUBLANE, LANE))
        for sg in range(n_sub):
            def conv_body(jg, c, lanes=lanes, wv=wv, bias=bias, sg=sg):
                base = pl.multiple_of(jg * (jb * S), jb * S)
                accs = [bias] * jb
                for t in range(jb + HALO):
                    row = vbuf[pl.ds(base + (t * S + sg * SUBLANE), SUBLANE), lanes]
                    for q in range(jb):
                        k = t - q
                        if 0 <= k < CONV_WIDTH:
                            accs[q] = accs[q] + wv[k] * row
                for q in range(jb):
                    proj[pl.ds(base + (q * S + sg * SUBLANE), SUBLANE), lanes] = accs[q]
                return c
            lax.fori_loop(0, m // jb, conv_body, 0)

    for rows in _pieces(TB, 64):
        h = _layer_norm(proj[rows, 0:D_CONV], clng_ref[...], clnb_ref[...])
        mixb[rows, D_SSM:D_MODEL] = (h * jax.nn.sigmoid(h)).astype(BF16)

    if chain:
        @pl.when(i == last)
        def _():
            for k in range(HALO):
                row = (m + k) * S + S - 1
                conv_out[k:k + 1, :] = vbuf[row:row + 1, :]
    else:
        conv_out[...] = vbuf[m * S:(m + HALO) * S, :]

    xb[...] = _bdot(permt_ref[...], mixb[...]).astype(BF16)
    proj[:, 0:D_MODEL] = _bdot(xb[...], wout_ref[...])
    for rows in _pieces(TB, LN_ROWS):
        x1_ref[rows, :] = _layer_norm(ALPHA * x_ref[rows, :] + proj[rows, 0:D_MODEL],
                                      ln1g_ref[...], ln1b_ref[...])


def _const_spec(shape):
    nd = len(shape)
    return pl.BlockSpec(shape, lambda b, i: (0,) * nd, pipeline_mode=pl.Buffered(1))


def _interleave_perm(S, m):
    r = jnp.arange(S * m)
    p = jax.nn.one_hot((r % S) * m + r // S, S * m, dtype=BF16)
    return p, p.T


def _mixer(x, h0r, h0i, cache, wts, *, S, m, chain):
    nb, T, _ = x.shape
    TB = S * m
    nblk = T // TB
    assert T == nblk * TB and S % SUBLANE == 0 and m % 8 == 0
    assert (not chain) or (S == SUBLANE and m >= HALO)
    s0 = h0r.shape[1]
    crow = cache.shape[1]
    bspec = lambda shape: pl.BlockSpec((None,) + shape, lambda b, i: (b, 0, 0))
    xspec = pl.BlockSpec((None, TB, D_MODEL), lambda b, i: (b, i, 0))
    wts = _interleave_perm(S, m) + tuple(wts)
    in_specs = [xspec, bspec((s0, N_STATE)), bspec((s0, N_STATE)), bspec((crow, D_CONV))]
    in_specs += [_const_spec(w.shape) for w in wts]
    out_specs = [xspec, bspec((s0, N_STATE)), bspec((s0, N_STATE)), bspec((crow, D_CONV))]
    out_shape = [jax.ShapeDtypeStruct((nb, T, D_MODEL), F32),
                 jax.ShapeDtypeStruct((nb, s0, N_STATE), F32),
                 jax.ShapeDtypeStruct((nb, s0, N_STATE), F32),
                 jax.ShapeDtypeStruct((nb, crow, D_CONV), F32)]
    scratch = [pltpu.VMEM((TB, D_MODEL), BF16),
               pltpu.VMEM((TB, D_MODEL), BF16),
               pltpu.VMEM((TB, D_IN), F32),
               pltpu.VMEM((TB, D_SSM), BF16),
               pltpu.VMEM((TB, 2 * N_STATE), F32),
               pltpu.VMEM((TB, 2 * N_STATE), BF16),
               pltpu.VMEM(((m + HALO) * S, D_CONV), F32),
               pltpu.VMEM((HALO * S, D_CONV), F32),
               pltpu.VMEM((TB, D_MODEL), BF16),
               pltpu.VMEM((S, N_STATE), F32),
               pltpu.VMEM((S, N_STATE), F32),
               pltpu.VMEM((1, N_STATE), F32),
               pltpu.VMEM((1, N_STATE), F32)]
    return pl.pallas_call(
        functools.partial(_mixer_kernel, S=S, m=m, chain=chain),
        grid=(nb, nblk),
        in_specs=in_specs,
        out_specs=out_specs,
        out_shape=out_shape,
        scratch_shapes=scratch,
        compiler_params=pltpu.CompilerParams(dimension_semantics=("arbitrary", "arbitrary"),
                                             vmem_limit_bytes=VMEM_LIMIT),
        name="mixer_chain" if chain else "mixer_streams",
    )(x, h0r, h0i, cache, *wts)


def _attend(q, kb, vb):
    outs = []
    for h in range(MEM_HEADS):
        cols = slice(h * HEAD_DIM, (h + 1) * HEAD_DIM)
        s = lax.dot_general(q[:, cols], kb[:, cols], (((1,), (1,)), ((), ())),
                            preferred_element_type=F32) * (HEAD_DIM ** -0.5)
        e = jnp.exp(s - jnp.max(s, axis=-1, keepdims=True))
        p = (e / jnp.sum(e, axis=-1, keepdims=True)).astype(BF16)
        outs.append(_bdot(p, vb[:, cols]).astype(BF16))
    return jnp.concatenate(outs, axis=1)


def _post_attention(halves, x1_ref, ob, wo_ref, ln2g, ln2b, w1_ref, b1_ref, w2_ref, b2_ref, ln3g, ln3b,
                    out_ref, fbuf, x2buf, xb, hidb):
    for hs in halves:
        fbuf[hs, :] = _bdot(ob[hs, :], wo_ref[...])
    for hs in halves:
        for rows in _pieces(hs.stop - hs.start, LN_ROWS):
            rows = slice(hs.start + rows.start, hs.start + rows.stop)
            x2 = _layer_norm(ALPHA * x1_ref[rows, :] + fbuf[rows, :], ln2g[...], ln2b[...])
            x2buf[rows, :] = x2
            xb[rows, :] = x2.astype(BF16)
        for n in range(D_FF // D_MODEL):
            cols = slice(n * D_MODEL, (n + 1) * D_MODEL)
            a = jnp.maximum(_bdot(xb[hs, :], w1_ref[:, cols]) + b1_ref[:, cols], 0.0)
            hidb[hs, cols] = (a * a).astype(BF16)
    for hs in halves:
        fbuf[hs, :] = _bdot(hidb[hs, :], w2_ref[...])
        for rows in _pieces(hs.stop - hs.start, LN_ROWS):
            rows = slice(hs.start + rows.start, hs.start + rows.stop)
            out_ref[rows, :] = _layer_norm(ALPHA * x2buf[rows, :] + fbuf[rows, :] + b2_ref[...],
                                           ln3g[...], ln3b[...])


def _attn_mlp_prompt_kernel(x1_ref, k_ref, v_ref, wq_ref, wo_ref, ln2g, ln2b, w1_ref, b1_ref,
                            w2_ref, b2_ref, ln3g, ln3b, out_ref,
                            xb, qb, ob, kb, vb, fbuf, x2buf, hidb, *, blocks_per_stream):
    n_rows = x1_ref.shape[0]
    halves = _pieces(n_rows, n_rows // 2)

    @pl.when(pl.program_id(0) % blocks_per_stream == 0)
    def _():
        kb[...] = k_ref[...].astype(BF16)
        vb[...] = v_ref[...].astype(BF16)

    for rows in _pieces(n_rows, 64):
        xb[rows, :] = x1_ref[rows, :].astype(BF16)
    for hs in halves:
        qb[hs, :] = _bdot(xb[hs, :], wq_ref[...]).astype(BF16)
    for hs in halves:
        ob[hs, :] = _attend(qb[hs, :], kb, vb)
    _post_attention(halves, x1_ref, ob, wo_ref, ln2g, ln2b, w1_ref, b1_ref, w2_ref, b2_ref, ln3g, ln3b,
                    out_ref, fbuf, x2buf, xb, hidb)


def _attn_mlp_streams_kernel(x1_ref, k_ref, v_ref, wq_ref, wo_ref, ln2g, ln2b, w1_ref, b1_ref,
                             w2_ref, b2_ref, ln3g, ln3b, out_ref,
                             xb, qb, ob, kb, vb, fbuf, x2buf, hidb, *, t_len):
    n_rows = x1_ref.shape[0]
    s_id = pl.program_id(0)

    @pl.when(s_id == 0)
    def _():
        xb[...] = x1_ref[...].astype(BF16)
        qb[...] = _bdot(xb[...], wq_ref[...]).astype(BF16)

    kb[...] = k_ref[...].astype(BF16)
    vb[...] = v_ref[...].astype(BF16)
    rows = pl.ds(pl.multiple_of(s_id * t_len, t_len), t_len)
    ob[rows, :] = _attend(qb[rows, :], kb, vb)

    @pl.when(s_id == pl.num_programs(0) - 1)
    def _():
        _post_attention(_pieces(n_rows, n_rows), x1_ref, ob, wo_ref, ln2g, ln2b, w1_ref, b1_ref, w2_ref,
                        b2_ref, ln3g, ln3b, out_ref, fbuf, x2buf, xb, hidb)


def _attn_mlp_scratch(rows):
    return [pltpu.VMEM((rows, D_MODEL), BF16),
            pltpu.VMEM((rows, D_MODEL), BF16),
            pltpu.VMEM((rows, D_MODEL), BF16),
            pltpu.VMEM((N_MEM, D_MODEL), BF16),
            pltpu.VMEM((N_MEM, D_MODEL), BF16),
            pltpu.VMEM((rows, D_MODEL), F32),
            pltpu.VMEM((rows, D_MODEL), F32),
            pltpu.VMEM((rows, D_FF), BF16)]


def _w1(shape):
    nd = len(shape)
    return pl.BlockSpec(shape, lambda i: (0,) * nd, pipeline_mode=pl.Buffered(1))


def _attn_mlp_prompt(x1, mem_k, mem_v, wts, rows_per_stream):
    n = x1.shape[0]
    blk_per_stream = rows_per_stream // ROWS_B
    assert rows_per_stream % ROWS_B == 0
    kvspec = pl.BlockSpec((None, N_MEM, D_MODEL), lambda i: (i // blk_per_stream, 0, 0))
    xspec = pl.BlockSpec((ROWS_B, D_MODEL), lambda i: (i, 0))
    return pl.pallas_call(
        functools.partial(_attn_mlp_prompt_kernel, blocks_per_stream=blk_per_stream),
        grid=(n // ROWS_B,),
        in_specs=[xspec, kvspec, kvspec] + [_w1(w.shape) for w in wts],
        out_specs=xspec,
        out_shape=jax.ShapeDtypeStruct((n, D_MODEL), F32),
        scratch_shapes=_attn_mlp_scratch(ROWS_B),
        compiler_params=pltpu.CompilerParams(dimension_semantics=("arbitrary",),
                                             vmem_limit_bytes=VMEM_LIMIT),
        name="attn_mlp_prompt",
    )(x1, mem_k, mem_v, *wts)


def _attn_mlp_streams(x1, mem_k, mem_v, wts, t_len):
    n = x1.shape[0]
    n_streams = n // t_len
    kvspec = pl.BlockSpec((None, N_MEM, D_MODEL), lambda s: (s, 0, 0))
    xspec = pl.BlockSpec((n, D_MODEL), lambda s: (0, 0))
    return pl.pallas_call(
        functools.partial(_attn_mlp_streams_kernel, t_len=t_len),
        grid=(n_streams,),
        in_specs=[xspec, kvspec, kvspec] + [_w1(w.shape) for w in wts],
        out_specs=xspec,
        out_shape=jax.ShapeDtypeStruct((n, D_MODEL), F32),
        scratch_shapes=_attn_mlp_scratch(n),
        compiler_params=pltpu.CompilerParams(dimension_semantics=("arbitrary",),
                                             vmem_limit_bytes=VMEM_LIMIT),
        name="attn_mlp_streams",
    )(x1, mem_k, mem_v, *wts)


def _s5_tables(a_re, a_im, log_dt, b_re, b_im, c_re, c_im, n_steps):
    dt = jnp.exp(log_dt)[:, None]
    mag = jnp.exp(a_re * dt)
    ang = a_im * dt
    ab_re = mag * jnp.cos(ang)
    ab_im = mag * jnp.sin(ang)
    den = a_re * a_re + a_im * a_im
    p = ab_re - 1.0
    q = ab_im
    k_re = ((p * a_re + q * a_im) / den)[..., None]
    k_im = ((q * a_re - p * a_im) / den)[..., None]
    bb_re = k_re * b_re - k_im * b_im
    bb_im = k_re * b_im + k_im * b_re

    steps = jnp.arange(1, n_steps + 1, dtype=F32)[:, None, None]
    pmag = jnp.exp(a_re * dt * steps)
    pang = a_im * dt * steps
    pw_re = (pmag * jnp.cos(pang)).reshape(n_steps, N_STATE)
    pw_im = (pmag * jnp.sin(pang)).reshape(n_steps, N_STATE)
    pw_re = pw_re.at[0].set(ab_re.reshape(N_STATE))
    pw_im = pw_im.at[0].set(ab_im.reshape(N_STATE))

    eye = jnp.eye(N_GROUPS, dtype=F32)
    full = lambda b: jnp.einsum("gnp,gh->gphn", b, eye).reshape(D_SSM, N_STATE)
    fb_re, fb_im = full(bb_re), full(bb_im)
    wb = jnp.stack([
        jnp.concatenate([fb_re[(c // 4) * LANE:(c // 4 + 1) * LANE, c * LANE:(c + 1) * LANE],
                         fb_im[(c // 4) * LANE:(c // 4 + 1) * LANE, c * LANE:(c + 1) * LANE]], axis=1)
        for c in range(N_TILES)]).astype(BF16)
    fullc = lambda cm: jnp.einsum("gpn,gh->hngp", cm, eye).reshape(N_STATE, D_SSM)
    fc_re, fc_im = fullc(c_re), fullc(c_im)
    cw = jnp.stack([
        jnp.concatenate([fc_re[c * LANE:(c + 1) * LANE, (c // 4) * LANE:(c // 4 + 1) * LANE],
                         -fc_im[c * LANE:(c + 1) * LANE, (c // 4) * LANE:(c // 4 + 1) * LANE]], axis=0)
        for c in range(N_TILES)]).astype(BF16)
    return wb, cw, pw_re, pw_im


def kernel(x_prompt, x_sample, state_ssm_re, state_ssm_im, cache_conv, cache_mem_k, cache_mem_v, mem_prompt, w_in, ssm_a_re, ssm_a_im, ssm_log_dt, ssm_b_re, ssm_b_im, ssm_c_re, ssm_c_im, ssm_d, glu_w, glu_b, conv_w, conv_b, conv_ln_g, conv_ln_b, w_out, ln1_g, ln1_b, mem_w_q, mem_w_k, mem_w_v, mem_w_o, ln2_g, ln2_b, mlp_w1, mlp_b1, mlp_w2, mlp_b2, ln3_g, ln3_b):
    assert w_in.shape[0] == DEPTH == 1
    nb, t_prompt, _ = x_prompt.shape
    n_dec, t_dec, _ = x_sample.shape
    row = lambda v: v[0].reshape(1, -1)

    def mixer_weights(n_steps):
        wb, cw, pw_re, pw_im = _s5_tables(ssm_a_re[0], ssm_a_im[0], ssm_log_dt[0], ssm_b_re[0],
                                          ssm_b_im[0], ssm_c_re[0], ssm_c_im[0], n_steps)
        return (w_in[0].astype(BF16), wb, cw, pw_re, pw_im, row(ssm_d), glu_w[0].astype(BF16),
                row(glu_b), conv_w[0], row(conv_b), row(conv_ln_g), row(conv_ln_b),
                w_out[0].astype(BF16), row(ln1_g), row(ln1_b))

    post_weights = (mem_w_q[0].astype(BF16), mem_w_o[0].astype(BF16), row(ln2_g), row(ln2_b),
                    mlp_w1[0].astype(BF16), row(mlp_b1), mlp_w2[0].astype(BF16), row(mlp_b2),
                    row(ln3_g), row(ln3_b))

    mk, mv = _memory_kv(mem_prompt, mem_w_k[0].astype(BF16), mem_w_v[0].astype(BF16))
    zero_h = jnp.zeros((nb, 1, N_STATE), F32)
    zero_conv = jnp.zeros((nb, HALO, D_CONV), x_prompt.dtype)
    x1p, p_re, p_im, p_conv = _mixer(x_prompt, zero_h, zero_h, zero_conv, mixer_weights(PROMPT_M),
                                     S=PROMPT_S, m=PROMPT_M, chain=True)
    yp = _attn_mlp_prompt(x1p.reshape(nb * t_prompt, D_MODEL), mk, mv, post_weights, t_prompt)
    yp = yp.reshape(nb, t_prompt, D_MODEL)

    cache_sj = cache_conv[0].transpose(1, 0, 2).reshape(1, HALO * n_dec, D_CONV)
    x1s, s_re, s_im, s_conv = _mixer(
        x_sample.reshape(1, n_dec * t_dec, D_MODEL),
        state_ssm_re[0].reshape(1, n_dec, N_STATE), state_ssm_im[0].reshape(1, n_dec, N_STATE),
        cache_sj, mixer_weights(t_dec), S=n_dec, m=t_dec, chain=False)
    ys = _attn_mlp_streams(x1s.reshape(n_dec * t_dec, D_MODEL),
                           cache_mem_k[0].reshape(n_dec, N_MEM, D_MODEL),
                           cache_mem_v[0].reshape(n_dec, N_MEM, D_MODEL), post_weights, t_dec)
    ys = ys.reshape(n_dec, t_dec, D_MODEL)
    s_conv = s_conv.reshape(HALO, n_dec, D_CONV).transpose(1, 0, 2)

    st = lambda a, n: a.reshape(1, n, N_GROUPS, SSM_STATE)
    kv5 = lambda a: a.reshape(1, nb, N_MEM, MEM_HEADS, HEAD_DIM)
    return (yp, ys, st(p_re, nb), st(p_im, nb), p_conv.reshape(1, nb, HALO, D_CONV), kv5(mk), kv5(mv),
            st(s_re, n_dec), st(s_im, n_dec), s_conv.reshape(1, n_dec, HALO, D_CONV))
```

```python
import functools
import math

import jax
import jax.numpy as jnp
from jax import lax
from jax.experimental import pallas as pl
from jax.experimental.pallas import tpu as pltpu

F32 = jnp.float32
BF16 = jnp.bfloat16

D_MODEL = 1024
D_SSM = 512
D_CONV = 512
SSM_GROUP = 16
N_GROUPS = 32
SSM_STATE = 64
N_STATE = N_GROUPS * SSM_STATE
CONV_WIDTH = 31
HALO = CONV_WIDTH - 1
N_MEM = 256
MEM_HEADS = 4
HEAD_DIM = 256
D_FF = 4096
D_IN = D_SSM + 2 * D_CONV
LN_EPS = 1e-5
DEPTH = 1
ALPHA = (2.0 * DEPTH) ** 0.25

LANE = 128
SUBLANE = 8
MXU_ROWS = 256
N_TILES = N_STATE // LANE
VMEM_LIMIT = 60 * 1024 * 1024

PROMPT_S = 8
PROMPT_M = 32
LN_ROWS = 32
CONV_JB = 8
CONV_GROUPS_PER_TRIP = 2
N_CONV_TRIPS = 8
W1_CHUNK = D_FF // N_CONV_TRIPS
PASS = 64

MIXER_WEIGHTS = ("perm", "permt", "w_in", "wb", "cw", "pwr", "pwi", "d", "gluw", "glub", "convw",
                 "convb", "clng", "clnb", "wout", "ln1g", "ln1b")
POST_WEIGHTS = ("wq", "wo", "ln2g", "ln2b", "w1", "b1", "w2", "b2", "ln3g", "ln3b")
MIXER_SCRATCH = ("xb", "xp", "proj", "ub", "bu", "hb", "vbuf", "tail", "mixb", "ire", "iim", "cbuf",
                 "vblk")
CUT = "cut"
POST_SCRATCH = ("qb", "ob", "abuf", "fbuf", "hidb")


def _layer_norm(x, g, b):
    mu = jnp.mean(x, axis=-1, keepdims=True)
    xc = x - mu
    var = jnp.mean(xc * xc, axis=-1, keepdims=True)
    return xc * lax.rsqrt(var + LN_EPS) * g + b


def _gelu_tanh(x):
    return 0.5 * x * (1.0 + jnp.tanh(math.sqrt(2.0 / math.pi) * (x + 0.044715 * (x * x * x))))


def _pieces(n_rows, chunk):
    chunk = min(chunk, n_rows)
    return [slice(r0, r0 + chunk) for r0 in range(0, n_rows, chunk)]


def _bdot(a, b):
    return jnp.dot(a, b, preferred_element_type=F32)


def _kv_kernel(mem_ref, wk_ref, wv_ref, k_ref, v_ref, kb_ref, vb_ref):
    mb = mem_ref[...].astype(BF16)
    k = _bdot(mb, wk_ref[...])
    v = _bdot(mb, wv_ref[...])
    k_ref[...] = k
    v_ref[...] = v
    kb_ref[...] = k.astype(BF16)
    vb_ref[...] = v.astype(BF16)


def _memory_kv(mem, wk, wv):
    nb = mem.shape[0]
    wspec = pl.BlockSpec((D_MODEL, D_MODEL), lambda b: (0, 0), pipeline_mode=pl.Buffered(1))
    ospec = pl.BlockSpec((None, N_MEM, D_MODEL), lambda b: (b, 0, 0))
    return pl.pallas_call(
        _kv_kernel,
        grid=(nb,),
        in_specs=[pl.BlockSpec((None, N_MEM, D_MODEL), lambda b: (b, 0, 0)), wspec, wspec],
        out_specs=[ospec] * 4,
        out_shape=[jax.ShapeDtypeStruct((nb, N_MEM, D_MODEL), F32)] * 2
        + [jax.ShapeDtypeStruct((nb, N_MEM, D_MODEL), BF16)] * 2,
        compiler_params=pltpu.CompilerParams(dimension_semantics=("arbitrary",),
                                             vmem_limit_bytes=VMEM_LIMIT),
        name="memory_kv",
    )(mem, wk, wv)


def _mixer_units(x_ref, enter_r, enter_i, prev_tail, w, sc, x1_ref, x1b_ref, out, *, S, m, chain,
                 conv_partner=None):
    TB = S * m
    xb, xp, proj, ub, bu, hb, vbuf, mixb = (sc[k] for k in ("xb", "xp", "proj", "ub", "bu", "hb", "vbuf", "mixb"))
    cbuf, vblk = sc["cbuf"], sc["vblk"]
    pwr_ref, pwi_ref = w["pwr"], w["pwi"]
    n_sub = S // SUBLANE
    mp = TB // MXU_ROWS

    for rows in _pieces(TB, 64):
        xb[rows, :] = x_ref[rows, :].astype(BF16)
    xp[...] = _bdot(w["perm"][...], xb[...]).astype(BF16)
    yield 4 * mp * mp * PASS
    for n in range(D_IN // 512):
        proj[:, n * 512:(n + 1) * 512] = _bdot(xp[...], w["w_in"][:, n * 512:(n + 1) * 512])
        yield 8 * mp * PASS

    for rows in _pieces(TB, 64):
        ub[rows, :] = proj[rows, 0:D_SSM].astype(BF16)
        vbuf[HALO * S + rows.start:HALO * S + rows.stop, :] = (
            proj[rows, D_SSM:D_SSM + D_CONV] * jax.nn.sigmoid(proj[rows, D_SSM + D_CONV:D_IN]))
        yield 110

    if chain:
        sub = lax.broadcasted_iota(jnp.int32, (S, D_CONV), 0)
        for k in range(HALO):
            cur = vbuf[(m + k) * S:(m + k + 1) * S, :]
            prev = prev_tail[k * S:(k + 1) * S, :]
            vbuf[k * S:(k + 1) * S, :] = pltpu.roll(jnp.where(sub == S - 1, prev, cur), 1, 0)
    else:
        vbuf[0:HALO * S, :] = prev_tail[...]
    for rows in _pieces((m + HALO) * S, 64):
        for lb in range(D_CONV // LANE):
            vblk[lb, rows, :] = vbuf[rows, lb * LANE:(lb + 1) * LANE]
    yield 40 * n_sub + 60
    yield CUT

    jb = CONV_JB
    per_lb = n_sub * (m // jb)
    trips_per_lb = per_lb // CONV_GROUPS_PER_TRIP
    assert trips_per_lb * CONV_GROUPS_PER_TRIP == per_lb
    assert trips_per_lb * (D_CONV // LANE) == N_CONV_TRIPS

    def conv_trip(i, c):
        lb = i // trips_per_lb
        wv = [jnp.broadcast_to(w["convw"][lb, k:k + 1, :], (SUBLANE, LANE)) for k in range(CONV_WIDTH)]
        bias = jnp.broadcast_to(w["convb"][lb], (SUBLANE, LANE))
        for gq in range(CONV_GROUPS_PER_TRIP):
            r = (i % trips_per_lb) * CONV_GROUPS_PER_TRIP + gq
            base = pl.multiple_of((r % (m // jb)) * (jb * S) + (r // (m // jb)) * SUBLANE, SUBLANE)
            accs = [bias] * jb
            for t in range(jb + HALO):
                row = vblk[lb, pl.ds(base + t * S, SUBLANE), :]
                for q in range(jb):
                    k = t - q
                    if 0 <= k < CONV_WIDTH:
                        accs[q] = accs[q] + wv[k] * row
            for q in range(jb):
                cbuf[lb, pl.ds(base + q * S, SUBLANE), :] = accs[q]
        if conv_partner is not None:
            conv_partner(i)
        return c
    lax.fori_loop(0, N_CONV_TRIPS, conv_trip, 0, unroll=4)
    yield 141 * CONV_GROUPS_PER_TRIP * N_CONV_TRIPS
    yield CUT

    def bu_tile(c):
        kb = c // 4
        res = _bdot(ub[:, kb * LANE:(kb + 1) * LANE], w["wb"][c])
        bu[:, c * LANE:(c + 1) * LANE] = res[:, 0:LANE]
        bu[:, N_STATE + c * LANE:N_STATE + (c + 1) * LANE] = res[:, LANE:2 * LANE]

    lch = (8 * 1024) // S
    tiles_per_chunk = lch // LANE
    for lc in range(N_STATE // lch):
        l0 = lc * lch
        for c in range(lc * tiles_per_chunk, (lc + 1) * tiles_per_chunk):
            bu_tile(c)
            yield mp * PASS
        ar = jnp.broadcast_to(pwr_ref[0:1, l0:l0 + lch], (S, lch))
        ai = jnp.broadcast_to(pwi_ref[0:1, l0:l0 + lch], (S, lch))
        hr = jnp.zeros((S, lch), F32)
        hi = jnp.zeros((S, lch), F32)
        for j in range(m):
            rows = slice(j * S, (j + 1) * S)
            hr, hi = (ar * hr - ai * hi + bu[rows, l0:l0 + lch],
                      ar * hi + ai * hr + bu[rows, N_STATE + l0:N_STATE + l0 + lch])
            bu[rows, l0:l0 + lch] = hr
            bu[rows, N_STATE + l0:N_STATE + l0 + lch] = hi
            if j % 4 == 3:
                yield 90

    if chain:
        ire, iim = sc["ire"], sc["iim"]
        amr = pwr_ref[m - 1:m, :]
        ami = pwi_ref[m - 1:m, :]
        cr, ci = enter_r, enter_i
        ire[0:1, :] = cr
        iim[0:1, :] = ci
        for s in range(S):
            row = (m - 1) * S + s
            zr = bu[row:row + 1, 0:N_STATE]
            zi = bu[row:row + 1, N_STATE:2 * N_STATE]
            cr, ci = amr * cr - ami * ci + zr, amr * ci + ami * cr + zi
            if s < S - 1:
                ire[s + 1:s + 2, :] = cr
                iim[s + 1:s + 2, :] = ci
        out["state"] = (cr, ci)
        e_r_all, e_i_all = ire[...], iim[...]
        yield 300
    else:
        e_r_all, e_i_all = enter_r, enter_i

    jstep = max(1, 16 // S)
    for lc in range(N_STATE // 512):
        l0 = lc * 512
        e_r = e_r_all[:, l0:l0 + 512]
        e_i = e_i_all[:, l0:l0 + 512]
        for jp in range(m // jstep):
            hs_r, hs_i = [], []
            for q in range(jstep):
                j = jp * jstep + q
                rows = slice(j * S, (j + 1) * S)
                pr = pwr_ref[j:j + 1, l0:l0 + 512]
                pi = pwi_ref[j:j + 1, l0:l0 + 512]
                hs_r.append(bu[rows, l0:l0 + 512] + pr * e_r - pi * e_i)
                hs_i.append(bu[rows, N_STATE + l0:N_STATE + l0 + 512] + pr * e_i + pi * e_r)
            orow = slice(jp * jstep * S, (jp + 1) * jstep * S)
            hb[orow, l0:l0 + 512] = jnp.concatenate(hs_r, axis=0).astype(BF16)
            hb[orow, N_STATE + l0:N_STATE + l0 + 512] = jnp.concatenate(hs_i, axis=0).astype(BF16)
            if jp % 4 == 3:
                yield 130

    if not chain:
        rows = slice((m - 1) * S, m * S)
        pr = pwr_ref[m - 1:m, :]
        pi = pwi_ref[m - 1:m, :]
        out["state"] = (bu[rows, 0:N_STATE] + pr * e_r_all - pi * e_i_all,
                        bu[rows, N_STATE:2 * N_STATE] + pr * e_i_all + pi * e_r_all)

    y0 = D_SSM + D_CONV
    for ob in range(D_SSM // LANE):
        acc = None
        for c in range(4 * ob, 4 * ob + 4):
            lhs = jnp.concatenate([hb[:, c * LANE:(c + 1) * LANE],
                                   hb[:, N_STATE + c * LANE:N_STATE + (c + 1) * LANE]], axis=1)
            t = _bdot(lhs, w["cw"][c])
            acc = t if acc is None else acc + t
        proj[:, y0 + ob * LANE:y0 + (ob + 1) * LANE] = acc
        yield 4 * mp * PASS

    for rows in _pieces(TB, 64):
        z = _gelu_tanh(proj[rows, y0:y0 + D_SSM] + w["d"][...] * proj[rows, 0:D_SSM])
        proj[rows, y0:y0 + D_SSM] = z
        mixb[rows, 0:D_SSM] = z.astype(BF16)
        yield 110
    proj[:, D_SSM:D_SSM + D_CONV] = _bdot(mixb[:, 0:D_SSM], w["gluw"][...])
    yield 4 * mp * PASS
    for rows in _pieces(TB, 64):
        gate = jax.nn.sigmoid(proj[rows, D_SSM:D_SSM + D_CONV] + w["glub"][...])
        mixb[rows, 0:D_SSM] = (proj[rows, y0:y0 + D_SSM] * gate).astype(BF16)
        yield 90

    for rows in _pieces(TB, 64):
        conv = jnp.concatenate([cbuf[lb, rows, :] for lb in range(D_CONV // LANE)], axis=1)
        h = _layer_norm(conv, w["clng"][...], w["clnb"][...])
        mixb[rows, D_SSM:D_MODEL] = (h * jax.nn.sigmoid(h)).astype(BF16)
        yield 110
    yield CUT

    xb[...] = _bdot(w["permt"][...], mixb[...]).astype(BF16)
    yield 4 * mp * mp * PASS
    for n in range(D_MODEL // 512):
        proj[:, n * 512:(n + 1) * 512] = _bdot(xb[...], w["wout"][:, n * 512:(n + 1) * 512])
        yield 8 * mp * PASS
    for rows in _pieces(TB, LN_ROWS):
        x1 = _layer_norm(ALPHA * x_ref[rows, :] + proj[rows, 0:D_MODEL], w["ln1g"][...], w["ln1b"][...])
        x1_ref[rows, :] = x1
        if x1b_ref is not None:
            x1b_ref[rows, :] = x1.astype(BF16)
        yield 100


def _attend_units(qb, q_rows, n_rows, kb, vb, ob):
    mp = max(1, n_rows // MXU_ROWS)
    for h in range(MEM_HEADS):
        cols = slice(h * HEAD_DIM, (h + 1) * HEAD_DIM)
        s = lax.dot_general(qb[q_rows, cols], kb[:, cols], (((1,), (1,)), ((), ())),
                            preferred_element_type=F32) * (HEAD_DIM ** -0.5)
        e = jnp.exp(s - jnp.max(s, axis=-1, keepdims=True))
        p = (e / jnp.sum(e, axis=-1, keepdims=True)).astype(BF16)
        ob[q_rows, cols] = _bdot(p, vb[:, cols]).astype(BF16)
        yield 330 * mp


def _attn_out_units(n_rows, x1_ref, w, sc, x2buf, x2b):
    ob, abuf = sc["ob"], sc["abuf"]
    mp = max(1, n_rows // MXU_ROWS)
    for n in range(D_MODEL // 256):
        cols = slice(n * 256, (n + 1) * 256)
        abuf[:, cols] = _bdot(ob[...], w["wo"][:, cols])
        yield 4 * mp * PASS
    for rows in _pieces(n_rows, LN_ROWS):
        x2 = _layer_norm(ALPHA * x1_ref[rows, :] + abuf[rows, :], w["ln2g"][...], w["ln2b"][...])
        x2buf[rows, :] = x2
        x2b[rows, :] = x2.astype(BF16)
        yield 100


def _mlp_hidden_chunk(c, x2b, w, sc):
    a = jnp.maximum(_bdot(x2b[...], w["w1"][c]) + w["b1"][c], 0.0)
    sc["hidb"][c] = (a * a).astype(BF16)


def _mlp_units(n_rows, x2buf, w, sc, out_ref):
    fbuf, hidb = sc["fbuf"], sc["hidb"]
    mp = max(1, n_rows // MXU_ROWS)
    for n in range(D_MODEL // 256):
        cols = slice(n * 256, (n + 1) * 256)
        acc = None
        for c in range(N_CONV_TRIPS):
            t = _bdot(hidb[c], w["w2"][c * W1_CHUNK:(c + 1) * W1_CHUNK, cols])
            acc = t if acc is None else acc + t
        fbuf[:, cols] = acc
        yield 16 * mp * PASS
    yield CUT
    for rows in _pieces(n_rows, LN_ROWS):
        out_ref[rows, :] = _layer_norm(ALPHA * x2buf[rows, :] + fbuf[rows, :] + w["b2"][...],
                                       w["ln3g"][...], w["ln3b"][...])
        yield 100


def _drain(gen):
    for _ in gen:
        pass


def _phase(gen):
    for cost in gen:
        if cost == CUT:
            return
        yield cost


def _weave(a, b, total_a, total_b):
    a, b = _phase(a), _phase(b)
    ta = tb = 0.0
    live_a = live_b = True
    while live_a or live_b:
        if live_a and (ta <= tb or not live_b):
            try:
                ta += next(a) / total_a
            except StopIteration:
                live_a = False
        else:
            try:
                tb += next(b) / total_b
            except StopIteration:
                live_b = False


def _layer_kernel(*refs, S, m, n_blocks, blocks_per_stream):
    n_in = 6 + len(MIXER_WEIGHTS) + len(POST_WEIGHTS)
    x_ref, h0r_ref, h0i_ref, cache_ref, kb_ref, vb_ref = refs[:6]
    wm = dict(zip(MIXER_WEIGHTS, refs[6:6 + len(MIXER_WEIGHTS)]))
    wp = dict(zip(POST_WEIGHTS, refs[6 + len(MIXER_WEIGHTS):n_in]))
    out_ref, hr_out, hi_out, conv_out = refs[n_in:n_in + 4]
    names = MIXER_SCRATCH + POST_SCRATCH + ("x1f", "x1b", "x2f", "x2b", "car_r", "car_i", "fin_r",
                                            "fin_i", "fin_c")
    sc = dict(zip(names, refs[n_in + 4:]))
    vbuf, tail = sc["vbuf"], sc["tail"]
    TB = S * m

    g = pl.program_id(0)
    i = jnp.minimum(g, n_blocks - 1) % blocks_per_stream
    first = i == 0
    keep = jnp.logical_and(g < n_blocks, i == blocks_per_stream - 1)

    @pl.when(g == 0)
    def _():
        for k in ("vbuf", "x2f", "x2b", "car_r", "car_i", "fin_r", "fin_i", "fin_c"):
            sc[k][...] = jnp.zeros(sc[k].shape, sc[k].dtype)

    for rows in _pieces(TB, 64):
        sc["x2f"][1, rows, :] = sc["x2f"][0, rows, :]

    enter_r = jnp.where(first, h0r_ref[...], sc["car_r"][...])
    enter_i = jnp.where(first, h0i_ref[...], sc["car_i"][...])
    for k in range(HALO):
        rows = slice(k * S, (k + 1) * S)
        tail[rows, :] = jnp.where(first, jnp.broadcast_to(cache_ref[k:k + 1, :], (S, D_CONV)),
                                  vbuf[(m + k) * S:(m + k + 1) * S, :])

    res = {}

    def block_stream():
        yield from _mixer_units(x_ref, enter_r, enter_i, tail, wm, sc, sc["x1f"], sc["x1b"], res,
                                S=S, m=m, chain=True,
                                conv_partner=lambda c: _mlp_hidden_chunk(c, sc["x2b"], wp, sc))
        for n in range(MEM_HEADS):
            cols = slice(n * HEAD_DIM, (n + 1) * HEAD_DIM)
            sc["qb"][:, cols] = _bdot(sc["x1b"][...], wp["wq"][:, cols]).astype(BF16)
            yield 4 * (TB // MXU_ROWS) * PASS
        yield from _attend_units(sc["qb"], slice(None), TB, kb_ref, vb_ref, sc["ob"])
        yield from _attn_out_units(TB, sc["x1f"], wp, sc, sc["x2f"].at[0], sc["x2b"])

    block = block_stream()
    mlp = _mlp_units(TB, sc["x2f"].at[1], wp, sc, out_ref)
    _drain(_phase(block))
    _drain(_phase(block))
    _weave(mlp, block, 4100, 7400)
    _weave(block, mlp, 6500, 800)

    cr, ci = res["state"]
    sc["car_r"][...] = cr
    sc["car_i"][...] = ci
    fin_r = jnp.where(keep, cr, sc["fin_r"][...])
    fin_i = jnp.where(keep, ci, sc["fin_i"][...])
    sc["fin_r"][...] = fin_r
    sc["fin_i"][...] = fin_i
    hr_out[...] = fin_r
    hi_out[...] = fin_i
    for k in range(HALO):
        row = (m + k) * S + S - 1
        fc = jnp.where(keep, vbuf[row:row + 1, :], sc["fin_c"][k:k + 1, :])
        sc["fin_c"][k:k + 1, :] = fc
        conv_out[k:k + 1, :] = fc


def _layer_prompt(x, h0r, h0i, cache, kb, vb, mixer_w, post_w, *, S, m):
    nb, T, _ = x.shape
    TB = S * m
    bps = T // TB
    n_blocks = nb * bps
    assert T == bps * TB and S == SUBLANE and m >= HALO and m % CONV_JB == 0
    mix = lambda g: jnp.minimum(g, n_blocks - 1)
    post = lambda g: jnp.maximum(g - 1, 0)
    per_stream = lambda shape, blk: pl.BlockSpec((None,) + shape, lambda g: (blk(g) // bps, 0, 0))
    const = lambda a: pl.BlockSpec(a.shape, lambda g: (0,) * a.ndim, pipeline_mode=pl.Buffered(1))
    in_specs = [pl.BlockSpec((None, TB, D_MODEL), lambda g: (mix(g) // bps, mix(g) % bps, 0)),
                per_stream((1, N_STATE), mix), per_stream((1, N_STATE), mix),
                per_stream((HALO, D_CONV), mix),
                per_stream((N_MEM, D_MODEL), mix), per_stream((N_MEM, D_MODEL), mix)]
    in_specs += [const(a) for a in mixer_w + post_w]
    out_specs = [pl.BlockSpec((TB, D_MODEL), lambda g: (post(g), 0)),
                 per_stream((1, N_STATE), mix), per_stream((1, N_STATE), mix),
                 per_stream((HALO, D_CONV), mix)]
    out_shape = [jax.ShapeDtypeStruct((nb * T, D_MODEL), F32),
                 jax.ShapeDtypeStruct((nb, 1, N_STATE), F32),
                 jax.ShapeDtypeStruct((nb, 1, N_STATE), F32),
                 jax.ShapeDtypeStruct((nb, HALO, D_CONV), F32)]
    scratch = _mixer_scratch(S, m) + _post_scratch(TB) + [
        pltpu.VMEM((TB, D_MODEL), F32),
        pltpu.VMEM((TB, D_MODEL), BF16),
        pltpu.VMEM((2, TB, D_MODEL), F32),
        pltpu.VMEM((TB, D_MODEL), BF16),
        pltpu.VMEM((1, N_STATE), F32),
        pltpu.VMEM((1, N_STATE), F32),
        pltpu.VMEM((1, N_STATE), F32),
        pltpu.VMEM((1, N_STATE), F32),
        pltpu.VMEM((HALO, D_CONV), F32)]
    return pl.pallas_call(
        functools.partial(_layer_kernel, S=S, m=m, n_blocks=n_blocks, blocks_per_stream=bps),
        grid=(n_blocks + 1,),
        in_specs=in_specs,
        out_specs=out_specs,
        out_shape=out_shape,
        scratch_shapes=scratch,
        compiler_params=pltpu.CompilerParams(dimension_semantics=("arbitrary",),
                                             vmem_limit_bytes=VMEM_LIMIT),
        name="layer_prompt",
    )(x, h0r, h0i, cache, kb, vb, *mixer_w, *post_w)


def _mixer_scratch(S, m):
    TB = S * m
    return [pltpu.VMEM((TB, D_MODEL), BF16),
            pltpu.VMEM((TB, D_MODEL), BF16),
            pltpu.VMEM((TB, D_IN), F32),
            pltpu.VMEM((TB, D_SSM), BF16),
            pltpu.VMEM((TB, 2 * N_STATE), F32),
            pltpu.VMEM((TB, 2 * N_STATE), BF16),
            pltpu.VMEM(((m + HALO) * S, D_CONV), F32),
            pltpu.VMEM((HALO * S, D_CONV), F32),
            pltpu.VMEM((TB, D_MODEL), BF16),
            pltpu.VMEM((S, N_STATE), F32),
            pltpu.VMEM((S, N_STATE), F32),
            pltpu.VMEM((D_CONV // LANE, TB, LANE), F32),
            pltpu.VMEM((D_CONV // LANE, (m + HALO) * S, LANE), F32)]


def _post_scratch(rows):
    return [pltpu.VMEM((rows, D_MODEL), BF16),
            pltpu.VMEM((rows, D_MODEL), BF16),
            pltpu.VMEM((rows, D_MODEL), F32),
            pltpu.VMEM((rows, D_MODEL), F32),
            pltpu.VMEM((N_CONV_TRIPS, rows, W1_CHUNK), BF16)]


def _mixer_streams_kernel(*refs, S, m):
    x_ref, h0r_ref, h0i_ref, cache_ref = refs[:4]
    n_in = 4 + len(MIXER_WEIGHTS)
    wm = dict(zip(MIXER_WEIGHTS, refs[4:n_in]))
    x1_ref, hr_out, hi_out, conv_out = refs[n_in:n_in + 4]
    sc = dict(zip(MIXER_SCRATCH, refs[n_in + 4:]))
    res = {}
    _drain(_mixer_units(x_ref, h0r_ref[...], h0i_ref[...], cache_ref, wm, sc, x1_ref, None, res,
                        S=S, m=m, chain=False))
    hr, hi = res["state"]
    hr_out[...] = hr
    hi_out[...] = hi
    conv_out[...] = sc["vbuf"][m * S:(m + HALO) * S, :]


def _mixer_streams(x, h0r, h0i, cache, mixer_w, *, S, m):
    TB = S * m
    assert x.shape[0] == TB and S % SUBLANE == 0 and m % CONV_JB == 0
    full = lambda a: pl.BlockSpec(a.shape, lambda i: (0,) * a.ndim)
    ins = (x, h0r, h0i, cache) + tuple(mixer_w)
    out_shape = [jax.ShapeDtypeStruct((TB, D_MODEL), F32),
                 jax.ShapeDtypeStruct((S, N_STATE), F32),
                 jax.ShapeDtypeStruct((S, N_STATE), F32),
                 jax.ShapeDtypeStruct((HALO * S, D_CONV), F32)]
    return pl.pallas_call(
        functools.partial(_mixer_streams_kernel, S=S, m=m),
        grid=(1,),
        in_specs=[full(a) for a in ins],
        out_specs=[full(a) for a in out_shape],
        out_shape=out_shape,
        scratch_shapes=_mixer_scratch(S, m),
        compiler_params=pltpu.CompilerParams(dimension_semantics=("arbitrary",),
                                             vmem_limit_bytes=VMEM_LIMIT),
        name="mixer_streams",
    )(*ins)


def _attn_mlp_streams_kernel(*refs, t_len):
    x1_ref, k_ref, v_ref = refs[:3]
    n_in = 3 + len(POST_WEIGHTS)
    wp = dict(zip(POST_WEIGHTS, refs[3:n_in]))
    out_ref = refs[n_in]
    sc = dict(zip(POST_SCRATCH + ("kb", "vb", "x2f", "x2b"), refs[n_in + 1:]))
    n_rows = x1_ref.shape[0]
    s_id = pl.program_id(0)

    @pl.when(s_id == 0)
    def _():
        sc["qb"][...] = _bdot(x1_ref[...].astype(BF16), wp["wq"][...]).astype(BF16)

    sc["kb"][...] = k_ref[...].astype(BF16)
    sc["vb"][...] = v_ref[...].astype(BF16)
    rows = pl.ds(pl.multiple_of(s_id * t_len, t_len), t_len)
    _drain(_attend_units(sc["qb"], rows, t_len, sc["kb"], sc["vb"], sc["ob"]))

    @pl.when(s_id == pl.num_programs(0) - 1)
    def _():
        _drain(_attn_out_units(n_rows, x1_ref, wp, sc, sc["x2f"], sc["x2b"]))
        for c in range(N_CONV_TRIPS):
            _mlp_hidden_chunk(c, sc["x2b"], wp, sc)
        _drain(_mlp_units(n_rows, sc["x2f"], wp, sc, out_ref))


def _attn_mlp_streams(x1, mem_k, mem_v, post_w, t_len):
    n = x1.shape[0]
    n_streams = n // t_len
    kvspec = pl.BlockSpec((None, N_MEM, D_MODEL), lambda s: (s, 0, 0))
    xspec = pl.BlockSpec((n, D_MODEL), lambda s: (0, 0))
    const = lambda a: pl.BlockSpec(a.shape, lambda s: (0,) * a.ndim, pipeline_mode=pl.Buffered(1))
    return pl.pallas_call(
        functools.partial(_attn_mlp_streams_kernel, t_len=t_len),
        grid=(n_streams,),
        in_specs=[xspec, kvspec, kvspec] + [const(a) for a in post_w],
        out_specs=xspec,
        out_shape=jax.ShapeDtypeStruct((n, D_MODEL), F32),
        scratch_shapes=_post_scratch(n) + [pltpu.VMEM((N_MEM, D_MODEL), BF16)] * 2 + [
            pltpu.VMEM((n, D_MODEL), F32), pltpu.VMEM((n, D_MODEL), BF16)],
        compiler_params=pltpu.CompilerParams(dimension_semantics=("arbitrary",),
                                             vmem_limit_bytes=VMEM_LIMIT),
        name="attn_mlp_streams",
    )(x1, mem_k, mem_v, *post_w)


def _interleave_perm(S, m):
    r = jnp.arange(S * m)
    p = jax.nn.one_hot((r % S) * m + r // S, S * m, dtype=BF16)
    return p, p.T


def _s5_tables(a_re, a_im, log_dt, b_re, b_im, c_re, c_im, n_steps):
    dt = jnp.exp(log_dt)[:, None]
    mag = jnp.exp(a_re * dt)
    ang = a_im * dt
    ab_re = mag * jnp.cos(ang)
    ab_im = mag * jnp.sin(ang)
    den = a_re * a_re + a_im * a_im
    p = ab_re - 1.0
    q = ab_im
    k_re = ((p * a_re + q * a_im) / den)[..., None]
    k_im = ((q * a_re - p * a_im) / den)[..., None]
    bb_re = k_re * b_re - k_im * b_im
    bb_im = k_re * b_im + k_im * b_re

    steps = jnp.arange(1, n_steps + 1, dtype=F32)[:, None, None]
    pmag = jnp.exp(a_re * dt * steps)
    pang = a_im * dt * steps
    pw_re = (pmag * jnp.cos(pang)).reshape(n_steps, N_STATE)
    pw_im = (pmag * jnp.sin(pang)).reshape(n_steps, N_STATE)
    pw_re = pw_re.at[0].set(ab_re.reshape(N_STATE))
    pw_im = pw_im.at[0].set(ab_im.reshape(N_STATE))

    eye = jnp.eye(N_GROUPS, dtype=F32)
    full = lambda b: jnp.einsum("gnp,gh->gphn", b, eye).reshape(D_SSM, N_STATE)
    fb_re, fb_im = full(bb_re), full(bb_im)
    wb = jnp.stack([
        jnp.concatenate([fb_re[(c // 4) * LANE:(c // 4 + 1) * LANE, c * LANE:(c + 1) * LANE],
                         fb_im[(c // 4) * LANE:(c // 4 + 1) * LANE, c * LANE:(c + 1) * LANE]], axis=1)
        for c in range(N_TILES)]).astype(BF16)
    fullc = lambda cm: jnp.einsum("gpn,gh->hngp", cm, eye).reshape(N_STATE, D_SSM)
    fc_re, fc_im = fullc(c_re), fullc(c_im)
    cw = jnp.stack([
        jnp.concatenate([fc_re[c * LANE:(c + 1) * LANE, (c // 4) * LANE:(c // 4 + 1) * LANE],
                         -fc_im[c * LANE:(c + 1) * LANE, (c // 4) * LANE:(c // 4 + 1) * LANE]], axis=0)
        for c in range(N_TILES)]).astype(BF16)
    return wb, cw, pw_re, pw_im


def kernel(x_prompt, x_sample, state_ssm_re, state_ssm_im, cache_conv, cache_mem_k, cache_mem_v, mem_prompt, w_in, ssm_a_re, ssm_a_im, ssm_log_dt, ssm_b_re, ssm_b_im, ssm_c_re, ssm_c_im, ssm_d, glu_w, glu_b, conv_w, conv_b, conv_ln_g, conv_ln_b, w_out, ln1_g, ln1_b, mem_w_q, mem_w_k, mem_w_v, mem_w_o, ln2_g, ln2_b, mlp_w1, mlp_b1, mlp_w2, mlp_b2, ln3_g, ln3_b):
    assert w_in.shape[0] == DEPTH == 1
    nb, t_prompt, _ = x_prompt.shape
    n_dec, t_dec, _ = x_sample.shape
    row = lambda v: v[0].reshape(1, -1)
    blocks = lambda a, width: a.reshape(a.shape[0], -1, width).transpose(1, 0, 2)
    lane_blocks = lambda a: blocks(a, LANE)
    col_chunks = lambda a: blocks(a, W1_CHUNK)

    def mixer_weights(S, m):
        wb, cw, pw_re, pw_im = _s5_tables(ssm_a_re[0], ssm_a_im[0], ssm_log_dt[0], ssm_b_re[0],
                                          ssm_b_im[0], ssm_c_re[0], ssm_c_im[0], m)
        return _interleave_perm(S, m) + (
            w_in[0].astype(BF16), wb, cw, pw_re, pw_im, row(ssm_d), glu_w[0].astype(BF16),
            row(glu_b), lane_blocks(conv_w[0]), lane_blocks(row(conv_b)), row(conv_ln_g), row(conv_ln_b),
            w_out[0].astype(BF16), row(ln1_g), row(ln1_b))

    post_weights = (mem_w_q[0].astype(BF16), mem_w_o[0].astype(BF16), row(ln2_g), row(ln2_b),
                    col_chunks(mlp_w1[0].astype(BF16)), col_chunks(row(mlp_b1)),
                    mlp_w2[0].astype(BF16), row(mlp_b2),
                    row(ln3_g), row(ln3_b))

    mk, mv, mkb, mvb = _memory_kv(mem_prompt, mem_w_k[0].astype(BF16), mem_w_v[0].astype(BF16))
    zero_h = jnp.zeros((nb, 1, N_STATE), F32)
    zero_conv = jnp.zeros((nb, HALO, D_CONV), x_prompt.dtype)
    yp, p_re, p_im, p_conv = _layer_prompt(x_prompt, zero_h, zero_h, zero_conv, mkb, mvb,
                                           mixer_weights(PROMPT_S, PROMPT_M), post_weights,
                                           S=PROMPT_S, m=PROMPT_M)
    yp = yp.reshape(nb, t_prompt, D_MODEL)

    cache_sj = cache_conv[0].transpose(1, 0, 2).reshape(HALO * n_dec, D_CONV)
    x1s, s_re, s_im, s_conv = _mixer_streams(
        x_sample.reshape(n_dec * t_dec, D_MODEL), state_ssm_re[0].reshape(n_dec, N_STATE),
        state_ssm_im[0].reshape(n_dec, N_STATE), cache_sj, mixer_weights(n_dec, t_dec),
        S=n_dec, m=t_dec)
    ys = _attn_mlp_streams(x1s, cache_mem_k[0].reshape(n_dec, N_MEM, D_MODEL),
                           cache_mem_v[0].reshape(n_dec, N_MEM, D_MODEL), post_weights, t_dec)
    ys = ys.reshape(n_dec, t_dec, D_MODEL)
    s_conv = s_conv.reshape(HALO, n_dec, D_CONV).transpose(1, 0, 2)

    st = lambda a, n: a.reshape(1, n, N_GROUPS, SSM_STATE)
    kv5 = lambda a: a.reshape(1, nb, N_MEM, MEM_HEADS, HEAD_DIM)
    return (yp, ys, st(p_re, nb), st(p_im, nb), p_conv.reshape(1, nb, HALO, D_CONV), kv5(mk), kv5(mv),
            st(s_re, n_dec), st(s_im, n_dec), s_conv.reshape(1, n_dec, HALO, D_CONV))
```

```python
import functools
import math

import jax
import jax.numpy as jnp
from jax import lax
from jax.experimental import pallas as pl
from jax.experimental.pallas import tpu as pltpu

F32 = jnp.float32
BF16 = jnp.bfloat16

D_MODEL = 1024
D_SSM = 512
D_CONV = 512
SSM_GROUP = 16
N_GROUPS = 32
SSM_STATE = 64
N_STATE = N_GROUPS * SSM_STATE
CONV_WIDTH = 31
HALO = CONV_WIDTH - 1
N_MEM = 256
MEM_HEADS = 4
HEAD_DIM = 256
D_FF = 4096
D_IN = D_SSM + 2 * D_CONV
LN_EPS = 1e-5
DEPTH = 1
ALPHA = (2.0 * DEPTH) ** 0.25

LANE = 128
SUBLANE = 8
BF16_ROWS = 16
N_TILES = N_STATE // LANE
VMEM_LIMIT = 56 * 1024 * 1024

PROMPT_S = 8
PROMPT_M = 64
ROWS_B = 512
LN_ROWS = 32
CONV_JB = 8


def _layer_norm(x, g, b):
    mu = jnp.mean(x, axis=-1, keepdims=True)
    xc = x - mu
    var = jnp.mean(xc * xc, axis=-1, keepdims=True)
    return xc * lax.rsqrt(var + LN_EPS) * g + b


def _gelu_tanh(x):
    return 0.5 * x * (1.0 + jnp.tanh(math.sqrt(2.0 / math.pi) * (x + 0.044715 * (x * x * x))))


def _pieces(n_rows, chunk):
    chunk = min(chunk, n_rows)
    return [slice(r0, r0 + chunk) for r0 in range(0, n_rows, chunk)]


def _bdot(a, b):
    return jnp.dot(a, b, preferred_element_type=F32)


def _kv_kernel(mem_ref, wk_ref, wv_ref, k_ref, v_ref):
    mb = mem_ref[...].astype(BF16)
    k_ref[...] = _bdot(mb, wk_ref[...])
    v_ref[...] = _bdot(mb, wv_ref[...])


def _memory_kv(mem, wk, wv):
    nb = mem.shape[0]
    wspec = pl.BlockSpec((D_MODEL, D_MODEL), lambda b: (0, 0), pipeline_mode=pl.Buffered(1))
    ospec = pl.BlockSpec((None, N_MEM, D_MODEL), lambda b: (b, 0, 0))
    return pl.pallas_call(
        _kv_kernel,
        grid=(nb,),
        in_specs=[pl.BlockSpec((None, N_MEM, D_MODEL), lambda b: (b, 0, 0)), wspec, wspec],
        out_specs=[ospec, ospec],
        out_shape=[jax.ShapeDtypeStruct((nb, N_MEM, D_MODEL), F32)] * 2,
        compiler_params=pltpu.CompilerParams(dimension_semantics=("arbitrary",),
                                             vmem_limit_bytes=VMEM_LIMIT),
        name="memory_kv",
    )(mem, wk, wv)


def _mixer_kernel(x_ref, h0r_ref, h0i_ref, cache_ref, perm_ref, permt_ref, w_in_ref, wb_ref, cw_ref,
                  pwr_ref, pwi_ref, pwbr_ref, pwbi_ref, d_ref, gluw_ref, glub_ref, convw_ref, convb_ref, clng_ref, clnb_ref,
                  wout_ref, ln1g_ref, ln1b_ref,
                  x1_ref, hr_out, hi_out, conv_out,
                  xb, xp, proj, ub, bu, hb, vbuf, tail, mixb, ire, iim, car_r, car_i,
                  *, S, m, chain):
    TB = S * m
    i = pl.program_id(1)
    last = pl.num_programs(1) - 1
    n_sub = S // SUBLANE

    for rows in _pieces(TB, 64):
        xb[rows, :] = x_ref[rows, :].astype(BF16)
    xp[...] = _bdot(perm_ref[...], xb[...]).astype(BF16)
    proj[...] = _bdot(xp[...], w_in_ref[...])

    if chain:
        @pl.when(i == 0)
        def _():
            for k in range(HALO):
                tail[k * S:(k + 1) * S, :] = jnp.broadcast_to(cache_ref[k:k + 1, :], (S, D_CONV))
            ire[0:1, :] = h0r_ref[...]
            iim[0:1, :] = h0i_ref[...]

        @pl.when(i > 0)
        def _():
            tail[...] = vbuf[m * S:(m + HALO) * S, :]
            ire[0:1, :] = car_r[...]
            iim[0:1, :] = car_i[...]

    for rows in _pieces(TB, 64):
        ub[rows, :] = proj[rows, 0:D_SSM].astype(BF16)
        vbuf[HALO * S + rows.start:HALO * S + rows.stop, :] = (
            proj[rows, D_SSM:D_SSM + D_CONV] * jax.nn.sigmoid(proj[rows, D_SSM + D_CONV:D_IN]))

    if chain:
        sub = lax.broadcasted_iota(jnp.int32, (S, D_CONV), 0)
        for k in range(HALO):
            cur = vbuf[(m + k) * S:(m + k + 1) * S, :]
            prev = tail[k * S:(k + 1) * S, :]
            vbuf[k * S:(k + 1) * S, :] = pltpu.roll(jnp.where(sub == S - 1, prev, cur), 1, 0)
    else:
        vbuf[0:HALO * S, :] = cache_ref[...]

    for c in range(N_TILES):
        kb = c // 4
        res = _bdot(ub[:, kb * LANE:(kb + 1) * LANE], wb_ref[c])
        bu[:, c * LANE:(c + 1) * LANE] = res[:, 0:LANE]
        bu[:, N_STATE + c * LANE:N_STATE + (c + 1) * LANE] = res[:, LANE:2 * LANE]

    lch = (8 * 1024) // S
    for lc in range(N_STATE // lch):
        l0 = lc * lch
        ar = jnp.broadcast_to(pwr_ref[0:1, l0:l0 + lch], (S, lch))
        ai = jnp.broadcast_to(pwi_ref[0:1, l0:l0 + lch], (S, lch))

        def scan_body(j, carry, l0=l0, ar=ar, ai=ai):
            hr, hi = carry
            rows = pl.ds(pl.multiple_of(j * S, S), S)
            nr = ar * hr - ai * hi + bu[rows, l0:l0 + lch]
            ni = ar * hi + ai * hr + bu[rows, N_STATE + l0:N_STATE + l0 + lch]
            bu[rows, l0:l0 + lch] = nr
            bu[rows, N_STATE + l0:N_STATE + l0 + lch] = ni
            return nr, ni
        z = jnp.zeros((S, lch), F32)
        lax.fori_loop(0, m, scan_body, (z, z), unroll=2)

    if chain:
        amr = pwr_ref[m - 1:m, :]
        ami = pwi_ref[m - 1:m, :]
        cr = ire[0:1, :]
        ci = iim[0:1, :]
        for s in range(S):
            row = (m - 1) * S + s
            zr = bu[row:row + 1, 0:N_STATE]
            zi = bu[row:row + 1, N_STATE:2 * N_STATE]
            cr, ci = amr * cr - ami * ci + zr, amr * ci + ami * cr + zi
            if s < S - 1:
                ire[s + 1:s + 2, :] = cr
                iim[s + 1:s + 2, :] = ci
        car_r[...] = cr
        car_i[...] = ci

        @pl.when(i == last)
        def _():
            hr_out[...] = cr
            hi_out[...] = ci
    in_r = ire if chain else h0r_ref
    in_i = iim if chain else h0i_ref

    reps = BF16_ROWS // S if S < BF16_ROWS else 1
    tile_rows = reps * S
    for lc in range(N_STATE // 512):
        l0 = lc * 512
        e_r = jnp.concatenate([in_r[:, l0:l0 + 512]] * reps, axis=0).astype(BF16)
        e_i = jnp.concatenate([in_i[:, l0:l0 + 512]] * reps, axis=0).astype(BF16)

        def fix_body(jp, c, l0=l0, e_r=e_r, e_i=e_i):
            rows = pl.ds(pl.multiple_of(jp * tile_rows, tile_rows), tile_rows)
            pr = pwbr_ref[rows, l0:l0 + 512]
            pi = pwbi_ref[rows, l0:l0 + 512]
            hz_r = bu[rows, l0:l0 + 512].astype(BF16)
            hz_i = bu[rows, N_STATE + l0:N_STATE + l0 + 512].astype(BF16)
            hb[rows, l0:l0 + 512] = hz_r + (pr * e_r - pi * e_i)
            hb[rows, N_STATE + l0:N_STATE + l0 + 512] = hz_i + (pr * e_i + pi * e_r)
            return c
        lax.fori_loop(0, TB // tile_rows, fix_body, 0, unroll=2)

    if not chain:
        rows = slice((m - 1) * S, m * S)
        pr = pwr_ref[m - 1:m, :]
        pi = pwi_ref[m - 1:m, :]
        e_r = in_r[...]
        e_i = in_i[...]
        hr_out[...] = bu[rows, 0:N_STATE] + pr * e_r - pi * e_i
        hi_out[...] = bu[rows, N_STATE:2 * N_STATE] + pr * e_i + pi * e_r

    y0 = D_SSM + D_CONV
    for ob in range(D_SSM // LANE):
        acc = None
        for c in range(4 * ob, 4 * ob + 4):
            lhs = jnp.concatenate([hb[:, c * LANE:(c + 1) * LANE],
                                   hb[:, N_STATE + c * LANE:N_STATE + (c + 1) * LANE]], axis=1)
            t = _bdot(lhs, cw_ref[c])
            acc = t if acc is None else acc + t
        proj[:, y0 + ob * LANE:y0 + (ob + 1) * LANE] = acc

    for rows in _pieces(TB, 64):
        z = _gelu_tanh(proj[rows, y0:y0 + D_SSM] + d_ref[...] * proj[rows, 0:D_SSM])
        proj[rows, y0:y0 + D_SSM] = z
        mixb[rows, 0:D_SSM] = z.astype(BF16)
    proj[:, D_SSM:D_SSM + D_CONV] = _bdot(mixb[:, 0:D_SSM], gluw_ref[...])
    for rows in _pieces(TB, 64):
        gate = jax.nn.sigmoid(proj[rows, D_SSM:D_SSM + D_CONV] + glub_ref[...])
        mixb[rows, 0:D_SSM] = (proj[rows, y0:y0 + D_SSM] * gate).astype(BF16)

    jb = CONV_JB
    for lb in range(D_CONV // LANE):
        lanes = slice(lb * LANE, (lb + 1) * LANE)
        wv = [jnp.broadcast_to(convw_ref[k:k + 1, lanes], (SUBLANE, LANE)) for k in range(CONV_WIDTH)]
        bias = jnp.broadcast_to(convb_ref[:, lanes], (SUBLANE, LANE))
        for sg in range(n_sub):
            def conv_body(jg, c, lanes=lanes, wv=wv, bias=bias, sg=sg):
                base = pl.multiple_of(jg * (jb * S), jb * S)
                accs = [bias] * jb
                for t in range(jb + HALO):
                    row = vbuf[pl.ds(base + (t * S + sg * SUBLANE), SUBLANE), lanes]
                    for q in range(jb):
                        k = t - q
                        if 0 <= k < CONV_WIDTH:
                            accs[q] = accs[q] + wv[k] * row
                for q in range(jb):
                    proj[pl.ds(base + (q * S + sg * SUBLANE), SUBLANE), lanes] = accs[q]
                return c
            lax.fori_loop(0, m // jb, conv_body, 0)

    for rows in _pieces(TB, 64):
        h = _layer_norm(proj[rows, 0:D_CONV], clng_ref[...], clnb_ref[...])
        mixb[rows, D_SSM:D_MODEL] = (h * jax.nn.sigmoid(h)).astype(BF16)

    if chain:
        @pl.when(i == last)
        def _():
            for k in range(HALO):
                row = (m + k) * S + S - 1
                conv_out[k:k + 1, :] = vbuf[row:row + 1, :]
    else:
        conv_out[...] = vbuf[m * S:(m + HALO) * S, :]

    xb[...] = _bdot(permt_ref[...], mixb[...]).astype(BF16)
    proj[:, 0:D_MODEL] = _bdot(xb[...], wout_ref[...])
    for rows in _pieces(TB, LN_ROWS):
        x1_ref[rows, :] = _layer_norm(ALPHA * x_ref[rows, :] + proj[rows, 0:D_MODEL],
                                      ln1g_ref[...], ln1b_ref[...])


def _const_spec(shape):
    nd = len(shape)
    return pl.BlockSpec(shape, lambda b, i: (0,) * nd, pipeline_mode=pl.Buffered(1))


def _interleave_perm(S, m):
    r = jnp.arange(S * m)
    p = jax.nn.one_hot((r % S) * m + r // S, S * m, dtype=BF16)
    return p, p.T


def _mixer(x, h0r, h0i, cache, wts, *, S, m, chain):
    nb, T, _ = x.shape
    TB = S * m
    nblk = T // TB
    assert T == nblk * TB and S % SUBLANE == 0 and m % CONV_JB == 0
    assert (not chain) or (S == SUBLANE and m >= HALO)
    s0 = h0r.shape[1]
    crow = cache.shape[1]
    bspec = lambda shape: pl.BlockSpec((None,) + shape, lambda b, i: (b, 0, 0))
    xspec = pl.BlockSpec((None, TB, D_MODEL), lambda b, i: (b, i, 0))
    wts = _interleave_perm(S, m) + tuple(wts)
    in_specs = [xspec, bspec((s0, N_STATE)), bspec((s0, N_STATE)), bspec((crow, D_CONV))]
    in_specs += [_const_spec(w.shape) for w in wts]
    out_specs = [xspec, bspec((s0, N_STATE)), bspec((s0, N_STATE)), bspec((crow, D_CONV))]
    out_shape = [jax.ShapeDtypeStruct((nb, T, D_MODEL), F32),
                 jax.ShapeDtypeStruct((nb, s0, N_STATE), F32),
                 jax.ShapeDtypeStruct((nb, s0, N_STATE), F32),
                 jax.ShapeDtypeStruct((nb, crow, D_CONV), F32)]
    scratch = [pltpu.VMEM((TB, D_MODEL), BF16),
               pltpu.VMEM((TB, D_MODEL), BF16),
               pltpu.VMEM((TB, D_IN), F32),
               pltpu.VMEM((TB, D_SSM), BF16),
               pltpu.VMEM((TB, 2 * N_STATE), F32),
               pltpu.VMEM((TB, 2 * N_STATE), BF16),
               pltpu.VMEM(((m + HALO) * S, D_CONV), F32),
               pltpu.VMEM((HALO * S, D_CONV), F32),
               pltpu.VMEM((TB, D_MODEL), BF16),
               pltpu.VMEM((S, N_STATE), F32),
               pltpu.VMEM((S, N_STATE), F32),
               pltpu.VMEM((1, N_STATE), F32),
               pltpu.VMEM((1, N_STATE), F32)]
    return pl.pallas_call(
        functools.partial(_mixer_kernel, S=S, m=m, chain=chain),
        grid=(nb, nblk),
        in_specs=in_specs,
        out_specs=out_specs,
        out_shape=out_shape,
        scratch_shapes=scratch,
        compiler_params=pltpu.CompilerParams(dimension_semantics=("arbitrary", "arbitrary"),
                                             vmem_limit_bytes=VMEM_LIMIT),
        name="mixer_chain" if chain else "mixer_streams",
    )(x, h0r, h0i, cache, *wts)


def _attend(q, kb, vb):
    outs = []
    for h in range(MEM_HEADS):
        cols = slice(h * HEAD_DIM, (h + 1) * HEAD_DIM)
        s = lax.dot_general(q[:, cols], kb[:, cols], (((1,), (1,)), ((), ())),
                            preferred_element_type=F32) * (HEAD_DIM ** -0.5)
        e = jnp.exp(s - jnp.max(s, axis=-1, keepdims=True))
        p = (e / jnp.sum(e, axis=-1, keepdims=True)).astype(BF16)
        outs.append(_bdot(p, vb[:, cols]).astype(BF16))
    return jnp.concatenate(outs, axis=1)


def _post_attention(halves, x1_ref, ob, wo_ref, ln2g, ln2b, w1_ref, b1_ref, w2_ref, b2_ref, ln3g, ln3b,
                    out_ref, fbuf, x2buf, xb, hidb):
    for hs in halves:
        fbuf[hs, :] = _bdot(ob[hs, :], wo_ref[...])
    for hs in halves:
        for rows in _pieces(hs.stop - hs.start, LN_ROWS):
            rows = slice(hs.start + rows.start, hs.start + rows.stop)
            x2 = _layer_norm(ALPHA * x1_ref[rows, :] + fbuf[rows, :], ln2g[...], ln2b[...])
            x2buf[rows, :] = x2
            xb[rows, :] = x2.astype(BF16)
        for n in range(D_FF // D_MODEL):
            cols = slice(n * D_MODEL, (n + 1) * D_MODEL)
            a = jnp.maximum(_bdot(xb[hs, :], w1_ref[:, cols]) + b1_ref[:, cols], 0.0)
            hidb[hs, cols] = (a * a).astype(BF16)
    for hs in halves:
        fbuf[hs, :] = _bdot(hidb[hs, :], w2_ref[...])
        for rows in _pieces(hs.stop - hs.start, LN_ROWS):
            rows = slice(hs.start + rows.start, hs.start + rows.stop)
            out_ref[rows, :] = _layer_norm(ALPHA * x2buf[rows, :] + fbuf[rows, :] + b2_ref[...],
                                           ln3g[...], ln3b[...])


def _attn_mlp_prompt_kernel(x1_ref, k_ref, v_ref, wq_ref, wo_ref, ln2g, ln2b, w1_ref, b1_ref,
                            w2_ref, b2_ref, ln3g, ln3b, out_ref,
                            xb, qb, ob, kb, vb, fbuf, x2buf, hidb, *, blocks_per_stream):
    n_rows = x1_ref.shape[0]
    halves = _pieces(n_rows, n_rows // 2)

    @pl.when(pl.program_id(0) % blocks_per_stream == 0)
    def _():
        kb[...] = k_ref[...].astype(BF16)
        vb[...] = v_ref[...].astype(BF16)

    for rows in _pieces(n_rows, 64):
        xb[rows, :] = x1_ref[rows, :].astype(BF16)
    for hs in halves:
        qb[hs, :] = _bdot(xb[hs, :], wq_ref[...]).astype(BF16)
    for hs in halves:
        ob[hs, :] = _attend(qb[hs, :], kb, vb)
    _post_attention(halves, x1_ref, ob, wo_ref, ln2g, ln2b, w1_ref, b1_ref, w2_ref, b2_ref, ln3g, ln3b,
                    out_ref, fbuf, x2buf, xb, hidb)


def _attn_mlp_streams_kernel(x1_ref, k_ref, v_ref, wq_ref, wo_ref, ln2g, ln2b, w1_ref, b1_ref,
                             w2_ref, b2_ref, ln3g, ln3b, out_ref,
                             xb, qb, ob, kb, vb, fbuf, x2buf, hidb, *, t_len):
    n_rows = x1_ref.shape[0]
    s_id = pl.program_id(0)

    @pl.when(s_id == 0)
    def _():
        xb[...] = x1_ref[...].astype(BF16)
        qb[...] = _bdot(xb[...], wq_ref[...]).astype(BF16)

    kb[...] = k_ref[...].astype(BF16)
    vb[...] = v_ref[...].astype(BF16)
    rows = pl.ds(pl.multiple_of(s_id * t_len, t_len), t_len)
    ob[rows, :] = _attend(qb[rows, :], kb, vb)

    @pl.when(s_id == pl.num_programs(0) - 1)
    def _():
        _post_attention(_pieces(n_rows, n_rows), x1_ref, ob, wo_ref, ln2g, ln2b, w1_ref, b1_ref, w2_ref,
                        b2_ref, ln3g, ln3b, out_ref, fbuf, x2buf, xb, hidb)


def _attn_mlp_scratch(rows):
    return [pltpu.VMEM((rows, D_MODEL), BF16),
            pltpu.VMEM((rows, D_MODEL), BF16),
            pltpu.VMEM((rows, D_MODEL), BF16),
            pltpu.VMEM((N_MEM, D_MODEL), BF16),
            pltpu.VMEM((N_MEM, D_MODEL), BF16),
            pltpu.VMEM((rows, D_MODEL), F32),
            pltpu.VMEM((rows, D_MODEL), F32),
            pltpu.VMEM((rows, D_FF), BF16)]


def _w1(shape):
    nd = len(shape)
    return pl.BlockSpec(shape, lambda i: (0,) * nd, pipeline_mode=pl.Buffered(1))


def _attn_mlp_prompt(x1, mem_k, mem_v, wts, rows_per_stream):
    n = x1.shape[0]
    blk_per_stream = rows_per_stream // ROWS_B
    assert rows_per_stream % ROWS_B == 0
    kvspec = pl.BlockSpec((None, N_MEM, D_MODEL), lambda i: (i // blk_per_stream, 0, 0))
    xspec = pl.BlockSpec((ROWS_B, D_MODEL), lambda i: (i, 0))
    return pl.pallas_call(
        functools.partial(_attn_mlp_prompt_kernel, blocks_per_stream=blk_per_stream),
        grid=(n // ROWS_B,),
        in_specs=[xspec, kvspec, kvspec] + [_w1(w.shape) for w in wts],
        out_specs=xspec,
        out_shape=jax.ShapeDtypeStruct((n, D_MODEL), F32),
        scratch_shapes=_attn_mlp_scratch(ROWS_B),
        compiler_params=pltpu.CompilerParams(dimension_semantics=("arbitrary",),
                                             vmem_limit_bytes=VMEM_LIMIT),
        name="attn_mlp_prompt",
    )(x1, mem_k, mem_v, *wts)


def _attn_mlp_streams(x1, mem_k, mem_v, wts, t_len):
    n = x1.shape[0]
    n_streams = n // t_len
    kvspec = pl.BlockSpec((None, N_MEM, D_MODEL), lambda s: (s, 0, 0))
    xspec = pl.BlockSpec((n, D_MODEL), lambda s: (0, 0))
    return pl.pallas_call(
        functools.partial(_attn_mlp_streams_kernel, t_len=t_len),
        grid=(n_streams,),
        in_specs=[xspec, kvspec, kvspec] + [_w1(w.shape) for w in wts],
        out_specs=xspec,
        out_shape=jax.ShapeDtypeStruct((n, D_MODEL), F32),
        scratch_shapes=_attn_mlp_scratch(n),
        compiler_params=pltpu.CompilerParams(dimension_semantics=("arbitrary",),
                                             vmem_limit_bytes=VMEM_LIMIT),
        name="attn_mlp_streams",
    )(x1, mem_k, mem_v, *wts)


def _s5_tables(a_re, a_im, log_dt, b_re, b_im, c_re, c_im, n_steps):
    dt = jnp.exp(log_dt)[:, None]
    mag = jnp.exp(a_re * dt)
    ang = a_im * dt
    ab_re = mag * jnp.cos(ang)
    ab_im = mag * jnp.sin(ang)
    den = a_re * a_re + a_im * a_im
    p = ab_re - 1.0
    q = ab_im
    k_re = ((p * a_re + q * a_im) / den)[..., None]
    k_im = ((q * a_re - p * a_im) / den)[..., None]
    bb_re = k_re * b_re - k_im * b_im
    bb_im = k_re * b_im + k_im * b_re

    steps = jnp.arange(1, n_steps + 1, dtype=F32)[:, None, None]
    pmag = jnp.exp(a_re * dt * steps)
    pang = a_im * dt * steps
    pw_re = (pmag * jnp.cos(pang)).reshape(n_steps, N_STATE)
    pw_im = (pmag * jnp.sin(pang)).reshape(n_steps, N_STATE)
    pw_re = pw_re.at[0].set(ab_re.reshape(N_STATE))
    pw_im = pw_im.at[0].set(ab_im.reshape(N_STATE))

    slot = jax.nn.one_hot(jnp.arange(N_GROUPS).reshape(N_TILES, 2) % 8, 8, dtype=F32)
    place_b = lambda b: jnp.einsum("cqnp,cqr->crpqn", b.reshape(N_TILES, 2, SSM_STATE, SSM_GROUP),
                                   slot).reshape(N_TILES, LANE, LANE)
    wb = jnp.concatenate([place_b(bb_re), place_b(bb_im)], axis=2).astype(BF16)
    place_c = lambda cm: jnp.einsum("cqpn,cqr->cqnrp", cm.reshape(N_TILES, 2, SSM_GROUP, SSM_STATE),
                                    slot).reshape(N_TILES, LANE, LANE)
    cw = jnp.concatenate([place_c(c_re), -place_c(c_im)], axis=1).astype(BF16)
    return wb, cw, pw_re, pw_im


def kernel(x_prompt, x_sample, state_ssm_re, state_ssm_im, cache_conv, cache_mem_k, cache_mem_v, mem_prompt, w_in, ssm_a_re, ssm_a_im, ssm_log_dt, ssm_b_re, ssm_b_im, ssm_c_re, ssm_c_im, ssm_d, glu_w, glu_b, conv_w, conv_b, conv_ln_g, conv_ln_b, w_out, ln1_g, ln1_b, mem_w_q, mem_w_k, mem_w_v, mem_w_o, ln2_g, ln2_b, mlp_w1, mlp_b1, mlp_w2, mlp_b2, ln3_g, ln3_b):
    assert w_in.shape[0] == DEPTH == 1
    nb, t_prompt, _ = x_prompt.shape
    n_dec, t_dec, _ = x_sample.shape
    row = lambda v: v[0].reshape(1, -1)

    wb, cw, pw_re, pw_im = _s5_tables(ssm_a_re[0], ssm_a_im[0], ssm_log_dt[0], ssm_b_re[0], ssm_b_im[0],
                                      ssm_c_re[0], ssm_c_im[0], max(PROMPT_M, t_dec))
    shared = (w_in[0].astype(BF16), wb, cw)
    rest = (row(ssm_d), glu_w[0].astype(BF16), row(glu_b), conv_w[0], row(conv_b), row(conv_ln_g),
            row(conv_ln_b), w_out[0].astype(BF16), row(ln1_g), row(ln1_b))
    per_row = lambda t, n_steps, S: jnp.repeat(t[:n_steps], S, axis=0).astype(BF16)
    mixer_weights = lambda n_steps, S: shared + (
        pw_re[:n_steps], pw_im[:n_steps], per_row(pw_re, n_steps, S), per_row(pw_im, n_steps, S)) + rest

    post_weights = (mem_w_q[0].astype(BF16), mem_w_o[0].astype(BF16), row(ln2_g), row(ln2_b),
                    mlp_w1[0].astype(BF16), row(mlp_b1), mlp_w2[0].astype(BF16), row(mlp_b2),
                    row(ln3_g), row(ln3_b))

    mk, mv = _memory_kv(mem_prompt, mem_w_k[0].astype(BF16), mem_w_v[0].astype(BF16))
    zero_h = jnp.zeros((nb, 1, N_STATE), F32)
    zero_conv = jnp.zeros((nb, HALO, D_CONV), x_prompt.dtype)
    x1p, p_re, p_im, p_conv = _mixer(x_prompt, zero_h, zero_h, zero_conv, mixer_weights(PROMPT_M, PROMPT_S),
                                     S=PROMPT_S, m=PROMPT_M, chain=True)
    yp = _attn_mlp_prompt(x1p.reshape(nb * t_prompt, D_MODEL), mk, mv, post_weights, t_prompt)
    yp = yp.reshape(nb, t_prompt, D_MODEL)

    cache_sj = cache_conv[0].transpose(1, 0, 2).reshape(1, HALO * n_dec, D_CONV)
    x1s, s_re, s_im, s_conv = _mixer(
        x_sample.reshape(1, n_dec * t_dec, D_MODEL),
        state_ssm_re[0].reshape(1, n_dec, N_STATE), state_ssm_im[0].reshape(1, n_dec, N_STATE),
        cache_sj, mixer_weights(t_dec, n_dec), S=n_dec, m=t_dec, chain=False)
    ys = _attn_mlp_streams(x1s.reshape(n_dec * t_dec, D_MODEL),
                           cache_mem_k[0].reshape(n_dec, N_MEM, D_MODEL),
                           cache_mem_v[0].reshape(n_dec, N_MEM, D_MODEL), post_weights, t_dec)
    ys = ys.reshape(n_dec, t_dec, D_MODEL)
    s_conv = s_conv.reshape(HALO, n_dec, D_CONV).transpose(1, 0, 2)

    st = lambda a, n: a.reshape(1, n, N_GROUPS, SSM_STATE)
    kv5 = lambda a: a.reshape(1, nb, N_MEM, MEM_HEADS, HEAD_DIM)
    return (yp, ys, st(p_re, nb), st(p_im, nb), p_conv.reshape(1, nb, HALO, D_CONV), kv5(mk), kv5(mv),
            st(s_re, n_dec), st(s_im, n_dec), s_conv.reshape(1, n_dec, HALO, D_CONV))
```

```python
import functools
import math

import jax
import jax.numpy as jnp
from jax import lax
from jax.experimental import pallas as pl
from jax.experimental.pallas import tpu as pltpu

F32 = jnp.float32
BF16 = jnp.bfloat16

D_MODEL = 1024
D_SSM = 512
D_CONV = 512
SSM_GROUP = 16
N_GROUPS = 32
SSM_STATE = 64
N_STATE = N_GROUPS * SSM_STATE
CONV_WIDTH = 31
HALO = CONV_WIDTH - 1
N_MEM = 256
MEM_HEADS = 4
HEAD_DIM = 256
D_FF = 4096
D_IN = D_SSM + 2 * D_CONV
LN_EPS = 1e-5
DEPTH = 1
ALPHA = (2.0 * DEPTH) ** 0.25

LANE = 128
SUBLANE = 8
BF16_ROWS = 16
N_TILES = N_STATE // LANE
VMEM_LIMIT = 56 * 1024 * 1024

PROMPT_S = 8
PROMPT_M = 64
ROWS_B = 512
LN_ROWS = 32
CONV_JB = 8
CONV_GROUPS_PER_TRIP = 2


def _layer_norm(x, g, b):
    mu = jnp.mean(x, axis=-1, keepdims=True)
    xc = x - mu
    var = jnp.mean(xc * xc, axis=-1, keepdims=True)
    return xc * lax.rsqrt(var + LN_EPS) * g + b


def _gelu_tanh(x):
    return 0.5 * x * (1.0 + jnp.tanh(math.sqrt(2.0 / math.pi) * (x + 0.044715 * (x * x * x))))


def _pieces(n_rows, chunk):
    chunk = min(chunk, n_rows)
    return [slice(r0, r0 + chunk) for r0 in range(0, n_rows, chunk)]


def _bdot(a, b):
    return jnp.dot(a, b, preferred_element_type=F32)


def _kv_kernel(mem_ref, wk_ref, wv_ref, k_ref, v_ref):
    mb = mem_ref[...].astype(BF16)
    k_ref[...] = _bdot(mb, wk_ref[...])
    v_ref[...] = _bdot(mb, wv_ref[...])


def _memory_kv(mem, wk, wv):
    nb = mem.shape[0]
    wspec = pl.BlockSpec((D_MODEL, D_MODEL), lambda b: (0, 0), pipeline_mode=pl.Buffered(1))
    ospec = pl.BlockSpec((None, N_MEM, D_MODEL), lambda b: (b, 0, 0))
    return pl.pallas_call(
        _kv_kernel,
        grid=(nb,),
        in_specs=[pl.BlockSpec((None, N_MEM, D_MODEL), lambda b: (b, 0, 0)), wspec, wspec],
        out_specs=[ospec, ospec],
        out_shape=[jax.ShapeDtypeStruct((nb, N_MEM, D_MODEL), F32)] * 2,
        compiler_params=pltpu.CompilerParams(dimension_semantics=("arbitrary",),
                                             vmem_limit_bytes=VMEM_LIMIT),
        name="memory_kv",
    )(mem, wk, wv)


def _mixer_kernel(x_ref, h0r_ref, h0i_ref, cache_ref, perm_ref, permt_ref, w_in_ref, wb_ref, cw_ref,
                  pwr_ref, pwi_ref, pwbr_ref, pwbi_ref, d_ref, gluw_ref, glub_ref, convw_ref, convb_ref, clng_ref, clnb_ref,
                  wout_ref, ln1g_ref, ln1b_ref,
                  x1_ref, hr_out, hi_out, conv_out,
                  xb, xp, proj, ub, bu, hb, vbuf, tail, mixb, ire, iim, car_r, car_i, vblk, cbuf,
                  *, S, m, chain):
    TB = S * m
    i = pl.program_id(1)
    last = pl.num_programs(1) - 1
    n_sub = S // SUBLANE

    for rows in _pieces(TB, 64):
        xb[rows, :] = x_ref[rows, :].astype(BF16)
    xp[...] = _bdot(perm_ref[...], xb[...]).astype(BF16)
    proj[...] = _bdot(xp[...], w_in_ref[...])

    if chain:
        @pl.when(i == 0)
        def _():
            for k in range(HALO):
                tail[k * S:(k + 1) * S, :] = jnp.broadcast_to(cache_ref[k:k + 1, :], (S, D_CONV))
            ire[0:1, :] = h0r_ref[...]
            iim[0:1, :] = h0i_ref[...]

        @pl.when(i > 0)
        def _():
            tail[...] = vbuf[m * S:(m + HALO) * S, :]
            ire[0:1, :] = car_r[...]
            iim[0:1, :] = car_i[...]

    for rows in _pieces(TB, 64):
        for kb in range(D_SSM // LANE):
            ub[kb, rows, :] = proj[rows, kb * LANE:(kb + 1) * LANE].astype(BF16)
        vbuf[HALO * S + rows.start:HALO * S + rows.stop, :] = (
            proj[rows, D_SSM:D_SSM + D_CONV] * jax.nn.sigmoid(proj[rows, D_SSM + D_CONV:D_IN]))

    if chain:
        sub = lax.broadcasted_iota(jnp.int32, (S, D_CONV), 0)
        for k in range(HALO):
            cur = vbuf[(m + k) * S:(m + k + 1) * S, :]
            prev = tail[k * S:(k + 1) * S, :]
            vbuf[k * S:(k + 1) * S, :] = pltpu.roll(jnp.where(sub == S - 1, prev, cur), 1, 0)
    else:
        vbuf[0:HALO * S, :] = cache_ref[...]
    for rows in _pieces((m + HALO) * S, 64):
        for lb in range(D_CONV // LANE):
            vblk[lb, rows, :] = vbuf[rows, lb * LANE:(lb + 1) * LANE]

    jb = CONV_JB
    per_lb = n_sub * (m // jb)
    trips_per_lb = per_lb // CONV_GROUPS_PER_TRIP
    n_trips = trips_per_lb * (D_CONV // LANE)
    tiles_per_trip = N_TILES // n_trips
    assert trips_per_lb * CONV_GROUPS_PER_TRIP == per_lb and tiles_per_trip * n_trips == N_TILES

    def conv_trip(i, c):
        lb = i // trips_per_lb
        wv = [jnp.broadcast_to(convw_ref[lb, k:k + 1, :], (SUBLANE, LANE)) for k in range(CONV_WIDTH)]
        bias = jnp.broadcast_to(convb_ref[lb], (SUBLANE, LANE))
        for gq in range(CONV_GROUPS_PER_TRIP):
            r = (i % trips_per_lb) * CONV_GROUPS_PER_TRIP + gq
            base = pl.multiple_of((r % (m // jb)) * (jb * S) + (r // (m // jb)) * SUBLANE, SUBLANE)
            accs = [bias] * jb
            for t in range(jb + HALO):
                row = vblk[lb, pl.ds(base + t * S, SUBLANE), :]
                for q in range(jb):
                    k = t - q
                    if 0 <= k < CONV_WIDTH:
                        accs[q] = accs[q] + wv[k] * row
            for q in range(jb):
                cbuf[lb, pl.ds(base + q * S, SUBLANE), :] = accs[q]
        for tq in range(tiles_per_trip):
            tile = i * tiles_per_trip + tq
            res = _bdot(ub[tile // 4], wb_ref[tile])
            bu[tile] = res[:, 0:LANE]
            bu[N_TILES + tile] = res[:, LANE:2 * LANE]
        return c
    lax.fori_loop(0, n_trips, conv_trip, 0, unroll=2)

    def bu_get(rows, t0, nt, part):
        return jnp.concatenate([bu[part * N_TILES + t, rows, :] for t in range(t0, t0 + nt)], axis=1)

    def bu_set(rows, t0, part, val):
        for idx in range(val.shape[1] // LANE):
            bu[part * N_TILES + t0 + idx, rows, :] = val[:, idx * LANE:(idx + 1) * LANE]

    lch = (8 * 1024) // S
    nt = lch // LANE
    for lc in range(N_STATE // lch):
        l0 = lc * lch
        ar = jnp.broadcast_to(pwr_ref[0:1, l0:l0 + lch], (S, lch))
        ai = jnp.broadcast_to(pwi_ref[0:1, l0:l0 + lch], (S, lch))

        def scan_body(j, carry, t0=lc * nt, ar=ar, ai=ai):
            hr, hi = carry
            rows = pl.ds(pl.multiple_of(j * S, S), S)
            nr = ar * hr - ai * hi + bu_get(rows, t0, nt, 0)
            ni = ar * hi + ai * hr + bu_get(rows, t0, nt, 1)
            bu_set(rows, t0, 0, nr)
            bu_set(rows, t0, 1, ni)
            return nr, ni
        z = jnp.zeros((S, lch), F32)
        lax.fori_loop(0, m, scan_body, (z, z), unroll=2)

    if chain:
        amr = pwr_ref[m - 1:m, :]
        ami = pwi_ref[m - 1:m, :]
        cr = ire[0:1, :]
        ci = iim[0:1, :]
        for s in range(S):
            row = (m - 1) * S + s
            zr = bu_get(slice(row, row + 1), 0, N_TILES, 0)
            zi = bu_get(slice(row, row + 1), 0, N_TILES, 1)
            cr, ci = amr * cr - ami * ci + zr, amr * ci + ami * cr + zi
            if s < S - 1:
                ire[s + 1:s + 2, :] = cr
                iim[s + 1:s + 2, :] = ci
        car_r[...] = cr
        car_i[...] = ci

        @pl.when(i == last)
        def _():
            hr_out[...] = cr
            hi_out[...] = ci
    in_r = ire if chain else h0r_ref
    in_i = iim if chain else h0i_ref

    reps = BF16_ROWS // S if S < BF16_ROWS else 1
    tile_rows = reps * S
    for lc in range(N_STATE // 512):
        l0 = lc * 512
        e_r = jnp.concatenate([in_r[:, l0:l0 + 512]] * reps, axis=0).astype(BF16)
        e_i = jnp.concatenate([in_i[:, l0:l0 + 512]] * reps, axis=0).astype(BF16)

        def fix_body(jp, c, l0=l0, e_r=e_r, e_i=e_i):
            rows = pl.ds(pl.multiple_of(jp * tile_rows, tile_rows), tile_rows)
            pr = pwbr_ref[rows, l0:l0 + 512]
            pi = pwbi_ref[rows, l0:l0 + 512]
            hz_r = bu_get(rows, l0 // LANE, 512 // LANE, 0).astype(BF16)
            hz_i = bu_get(rows, l0 // LANE, 512 // LANE, 1).astype(BF16)
            hb[rows, l0:l0 + 512] = hz_r + (pr * e_r - pi * e_i)
            hb[rows, N_STATE + l0:N_STATE + l0 + 512] = hz_i + (pr * e_i + pi * e_r)
            return c
        lax.fori_loop(0, TB // tile_rows, fix_body, 0, unroll=2)

    if not chain:
        rows = slice((m - 1) * S, m * S)
        pr = pwr_ref[m - 1:m, :]
        pi = pwi_ref[m - 1:m, :]
        e_r = in_r[...]
        e_i = in_i[...]
        hr_out[...] = bu_get(rows, 0, N_TILES, 0) + pr * e_r - pi * e_i
        hi_out[...] = bu_get(rows, 0, N_TILES, 1) + pr * e_i + pi * e_r

    y0 = D_SSM + D_CONV
    for ob in range(D_SSM // LANE):
        acc = None
        for c in range(4 * ob, 4 * ob + 4):
            lhs = jnp.concatenate([hb[:, c * LANE:(c + 1) * LANE],
                                   hb[:, N_STATE + c * LANE:N_STATE + (c + 1) * LANE]], axis=1)
            t = _bdot(lhs, cw_ref[c])
            acc = t if acc is None else acc + t
        proj[:, y0 + ob * LANE:y0 + (ob + 1) * LANE] = acc

    for rows in _pieces(TB, 64):
        z = _gelu_tanh(proj[rows, y0:y0 + D_SSM] + d_ref[...] * proj[rows, 0:D_SSM])
        proj[rows, y0:y0 + D_SSM] = z
        mixb[rows, 0:D_SSM] = z.astype(BF16)
    proj[:, D_SSM:D_SSM + D_CONV] = _bdot(mixb[:, 0:D_SSM], gluw_ref[...])
    for rows in _pieces(TB, 64):
        gate = jax.nn.sigmoid(proj[rows, D_SSM:D_SSM + D_CONV] + glub_ref[...])
        mixb[rows, 0:D_SSM] = (proj[rows, y0:y0 + D_SSM] * gate).astype(BF16)

    for rows in _pieces(TB, 64):
        conv = jnp.concatenate([cbuf[lb, rows, :] for lb in range(D_CONV // LANE)], axis=1)
        h = _layer_norm(conv, clng_ref[...], clnb_ref[...])
        mixb[rows, D_SSM:D_MODEL] = (h * jax.nn.sigmoid(h)).astype(BF16)

    if chain:
        @pl.when(i == last)
        def _():
            for k in range(HALO):
                row = (m + k) * S + S - 1
                conv_out[k:k + 1, :] = vbuf[row:row + 1, :]
    else:
        conv_out[...] = vbuf[m * S:(m + HALO) * S, :]

    xb[...] = _bdot(permt_ref[...], mixb[...]).astype(BF16)
    proj[:, 0:D_MODEL] = _bdot(xb[...], wout_ref[...])
    for rows in _pieces(TB, LN_ROWS):
        x1_ref[rows, :] = _layer_norm(ALPHA * x_ref[rows, :] + proj[rows, 0:D_MODEL],
                                      ln1g_ref[...], ln1b_ref[...])


def _const_spec(shape):
    nd = len(shape)
    return pl.BlockSpec(shape, lambda b, i: (0,) * nd, pipeline_mode=pl.Buffered(1))


def _interleave_perm(S, m):
    r = jnp.arange(S * m)
    p = jax.nn.one_hot((r % S) * m + r // S, S * m, dtype=BF16)
    return p, p.T


def _mixer(x, h0r, h0i, cache, wts, *, S, m, chain):
    nb, T, _ = x.shape
    TB = S * m
    nblk = T // TB
    assert T == nblk * TB and S % SUBLANE == 0 and m % CONV_JB == 0
    assert (not chain) or (S == SUBLANE and m >= HALO)
    s0 = h0r.shape[1]
    crow = cache.shape[1]
    bspec = lambda shape: pl.BlockSpec((None,) + shape, lambda b, i: (b, 0, 0))
    xspec = pl.BlockSpec((None, TB, D_MODEL), lambda b, i: (b, i, 0))
    wts = _interleave_perm(S, m) + tuple(wts)
    in_specs = [xspec, bspec((s0, N_STATE)), bspec((s0, N_STATE)), bspec((crow, D_CONV))]
    in_specs += [_const_spec(w.shape) for w in wts]
    out_specs = [xspec, bspec((s0, N_STATE)), bspec((s0, N_STATE)), bspec((crow, D_CONV))]
    out_shape = [jax.ShapeDtypeStruct((nb, T, D_MODEL), F32),
                 jax.ShapeDtypeStruct((nb, s0, N_STATE), F32),
                 jax.ShapeDtypeStruct((nb, s0, N_STATE), F32),
                 jax.ShapeDtypeStruct((nb, crow, D_CONV), F32)]
    scratch = [pltpu.VMEM((TB, D_MODEL), BF16),
               pltpu.VMEM((TB, D_MODEL), BF16),
               pltpu.VMEM((TB, D_IN), F32),
               pltpu.VMEM((D_SSM // LANE, TB, LANE), BF16),
               pltpu.VMEM((2 * N_TILES, TB, LANE), F32),
               pltpu.VMEM((TB, 2 * N_STATE), BF16),
               pltpu.VMEM(((m + HALO) * S, D_CONV), F32),
               pltpu.VMEM((HALO * S, D_CONV), F32),
               pltpu.VMEM((TB, D_MODEL), BF16),
               pltpu.VMEM((S, N_STATE), F32),
               pltpu.VMEM((S, N_STATE), F32),
               pltpu.VMEM((1, N_STATE), F32),
               pltpu.VMEM((1, N_STATE), F32),
               pltpu.VMEM((D_CONV // LANE, (m + HALO) * S, LANE), F32),
               pltpu.VMEM((D_CONV // LANE, TB, LANE), F32)]
    return pl.pallas_call(
        functools.partial(_mixer_kernel, S=S, m=m, chain=chain),
        grid=(nb, nblk),
        in_specs=in_specs,
        out_specs=out_specs,
        out_shape=out_shape,
        scratch_shapes=scratch,
        compiler_params=pltpu.CompilerParams(dimension_semantics=("arbitrary", "arbitrary"),
                                             vmem_limit_bytes=VMEM_LIMIT),
        name="mixer_chain" if chain else "mixer_streams",
    )(x, h0r, h0i, cache, *wts)


def _attend(q, kb, vb):
    outs = []
    for h in range(MEM_HEADS):
        cols = slice(h * HEAD_DIM, (h + 1) * HEAD_DIM)
        s = lax.dot_general(q[:, cols], kb[:, cols], (((1,), (1,)), ((), ())),
                            preferred_element_type=F32) * (HEAD_DIM ** -0.5)
        e = jnp.exp(s - jnp.max(s, axis=-1, keepdims=True))
        p = (e / jnp.sum(e, axis=-1, keepdims=True)).astype(BF16)
        outs.append(_bdot(p, vb[:, cols]).astype(BF16))
    return jnp.concatenate(outs, axis=1)


def _post_attention(halves, x1_ref, ob, wo_ref, ln2g, ln2b, w1_ref, b1_ref, w2_ref, b2_ref, ln3g, ln3b,
                    out_ref, fbuf, x2buf, xb, hidb):
    for hs in halves:
        fbuf[hs, :] = _bdot(ob[hs, :], wo_ref[...])
    for hs in halves:
        for rows in _pieces(hs.stop - hs.start, LN_ROWS):
            rows = slice(hs.start + rows.start, hs.start + rows.stop)
            x2 = _layer_norm(ALPHA * x1_ref[rows, :] + fbuf[rows, :], ln2g[...], ln2b[...])
            x2buf[rows, :] = x2
            xb[rows, :] = x2.astype(BF16)
        for n in range(D_FF // D_MODEL):
            cols = slice(n * D_MODEL, (n + 1) * D_MODEL)
            a = jnp.maximum(_bdot(xb[hs, :], w1_ref[:, cols]) + b1_ref[:, cols], 0.0)
            hidb[hs, cols] = (a * a).astype(BF16)
    for hs in halves:
        fbuf[hs, :] = _bdot(hidb[hs, :], w2_ref[...])
        for rows in _pieces(hs.stop - hs.start, LN_ROWS):
            rows = slice(hs.start + rows.start, hs.start + rows.stop)
            out_ref[rows, :] = _layer_norm(ALPHA * x2buf[rows, :] + fbuf[rows, :] + b2_ref[...],
                                           ln3g[...], ln3b[...])


def _attn_mlp_prompt_kernel(x1_ref, k_ref, v_ref, wq_ref, wo_ref, ln2g, ln2b, w1_ref, b1_ref,
                            w2_ref, b2_ref, ln3g, ln3b, out_ref,
                            xb, qb, ob, kb, vb, fbuf, x2buf, hidb, *, blocks_per_stream):
    n_rows = x1_ref.shape[0]
    halves = _pieces(n_rows, n_rows // 2)

    @pl.when(pl.program_id(0) % blocks_per_stream == 0)
    def _():
        kb[...] = k_ref[...].astype(BF16)
        vb[...] = v_ref[...].astype(BF16)

    for rows in _pieces(n_rows, 64):
        xb[rows, :] = x1_ref[rows, :].astype(BF16)
    for hs in halves:
        qb[hs, :] = _bdot(xb[hs, :], wq_ref[...]).astype(BF16)
    for hs in halves:
        ob[hs, :] = _attend(qb[hs, :], kb, vb)
    _post_attention(halves, x1_ref, ob, wo_ref, ln2g, ln2b, w1_ref, b1_ref, w2_ref, b2_ref, ln3g, ln3b,
                    out_ref, fbuf, x2buf, xb, hidb)


def _attn_mlp_streams_kernel(x1_ref, k_ref, v_ref, wq_ref, wo_ref, ln2g, ln2b, w1_ref, b1_ref,
                             w2_ref, b2_ref, ln3g, ln3b, out_ref,
                             xb, qb, ob, kb, vb, fbuf, x2buf, hidb, *, t_len):
    n_rows = x1_ref.shape[0]
    s_id = pl.program_id(0)

    @pl.when(s_id == 0)
    def _():
        xb[...] = x1_ref[...].astype(BF16)
        qb[...] = _bdot(xb[...], wq_ref[...]).astype(BF16)

    kb[...] = k_ref[...].astype(BF16)
    vb[...] = v_ref[...].astype(BF16)
    rows = pl.ds(pl.multiple_of(s_id * t_len, t_len), t_len)
    ob[rows, :] = _attend(qb[rows, :], kb, vb)

    @pl.when(s_id == pl.num_programs(0) - 1)
    def _():
        _post_attention(_pieces(n_rows, n_rows), x1_ref, ob, wo_ref, ln2g, ln2b, w1_ref, b1_ref, w2_ref,
                        b2_ref, ln3g, ln3b, out_ref, fbuf, x2buf, xb, hidb)


def _attn_mlp_scratch(rows):
    return [pltpu.VMEM((rows, D_MODEL), BF16),
            pltpu.VMEM((rows, D_MODEL), BF16),
            pltpu.VMEM((rows, D_MODEL), BF16),
            pltpu.VMEM((N_MEM, D_MODEL), BF16),
            pltpu.VMEM((N_MEM, D_MODEL), BF16),
            pltpu.VMEM((rows, D_MODEL), F32),
            pltpu.VMEM((rows, D_MODEL), F32),
            pltpu.VMEM((rows, D_FF), BF16)]


def _w1(shape):
    nd = len(shape)
    return pl.BlockSpec(shape, lambda i: (0,) * nd, pipeline_mode=pl.Buffered(1))


def _attn_mlp_prompt(x1, mem_k, mem_v, wts, rows_per_stream):
    n = x1.shape[0]
    blk_per_stream = rows_per_stream // ROWS_B
    assert rows_per_stream % ROWS_B == 0
    kvspec = pl.BlockSpec((None, N_MEM, D_MODEL), lambda i: (i // blk_per_stream, 0, 0))
    xspec = pl.BlockSpec((ROWS_B, D_MODEL), lambda i: (i, 0))
    return pl.pallas_call(
        functools.partial(_attn_mlp_prompt_kernel, blocks_per_stream=blk_per_stream),
        grid=(n // ROWS_B,),
        in_specs=[xspec, kvspec, kvspec] + [_w1(w.shape) for w in wts],
        out_specs=xspec,
        out_shape=jax.ShapeDtypeStruct((n, D_MODEL), F32),
        scratch_shapes=_attn_mlp_scratch(ROWS_B),
        compiler_params=pltpu.CompilerParams(dimension_semantics=("arbitrary",),
                                             vmem_limit_bytes=VMEM_LIMIT),
        name="attn_mlp_prompt",
    )(x1, mem_k, mem_v, *wts)


def _attn_mlp_streams(x1, mem_k, mem_v, wts, t_len):
    n = x1.shape[0]
    n_streams = n // t_len
    kvspec = pl.BlockSpec((None, N_MEM, D_MODEL), lambda s: (s, 0, 0))
    xspec = pl.BlockSpec((n, D_MODEL), lambda s: (0, 0))
    return pl.pallas_call(
        functools.partial(_attn_mlp_streams_kernel, t_len=t_len),
        grid=(n_streams,),
        in_specs=[xspec, kvspec, kvspec] + [_w1(w.shape) for w in wts],
        out_specs=xspec,
        out_shape=jax.ShapeDtypeStruct((n, D_MODEL), F32),
        scratch_shapes=_attn_mlp_scratch(n),
        compiler_params=pltpu.CompilerParams(dimension_semantics=("arbitrary",),
                                             vmem_limit_bytes=VMEM_LIMIT),
        name="attn_mlp_streams",
    )(x1, mem_k, mem_v, *wts)


def _s5_tables(a_re, a_im, log_dt, b_re, b_im, c_re, c_im, n_steps):
    dt = jnp.exp(log_dt)[:, None]
    mag = jnp.exp(a_re * dt)
    ang = a_im * dt
    ab_re = mag * jnp.cos(ang)
    ab_im = mag * jnp.sin(ang)
    den = a_re * a_re + a_im * a_im
    p = ab_re - 1.0
    q = ab_im
    k_re = ((p * a_re + q * a_im) / den)[..., None]
    k_im = ((q * a_re - p * a_im) / den)[..., None]
    bb_re = k_re * b_re - k_im * b_im
    bb_im = k_re * b_im + k_im * b_re

    steps = jnp.arange(1, n_steps + 1, dtype=F32)[:, None, None]
    pmag = jnp.exp(a_re * dt * steps)
    pang = a_im * dt * steps
    pw_re = (pmag * jnp.cos(pang)).reshape(n_steps, N_STATE)
    pw_im = (pmag * jnp.sin(pang)).reshape(n_steps, N_STATE)
    pw_re = pw_re.at[0].set(ab_re.reshape(N_STATE))
    pw_im = pw_im.at[0].set(ab_im.reshape(N_STATE))

    slot = jax.nn.one_hot(jnp.arange(N_GROUPS).reshape(N_TILES, 2) % 8, 8, dtype=F32)
    place_b = lambda b: jnp.einsum("cqnp,cqr->crpqn", b.reshape(N_TILES, 2, SSM_STATE, SSM_GROUP),
                                   slot).reshape(N_TILES, LANE, LANE)
    wb = jnp.concatenate([place_b(bb_re), place_b(bb_im)], axis=2).astype(BF16)
    place_c = lambda cm: jnp.einsum("cqpn,cqr->cqnrp", cm.reshape(N_TILES, 2, SSM_GROUP, SSM_STATE),
                                    slot).reshape(N_TILES, LANE, LANE)
    cw = jnp.concatenate([place_c(c_re), -place_c(c_im)], axis=1).astype(BF16)
    return wb, cw, pw_re, pw_im


def kernel(x_prompt, x_sample, state_ssm_re, state_ssm_im, cache_conv, cache_mem_k, cache_mem_v, mem_prompt, w_in, ssm_a_re, ssm_a_im, ssm_log_dt, ssm_b_re, ssm_b_im, ssm_c_re, ssm_c_im, ssm_d, glu_w, glu_b, conv_w, conv_b, conv_ln_g, conv_ln_b, w_out, ln1_g, ln1_b, mem_w_q, mem_w_k, mem_w_v, mem_w_o, ln2_g, ln2_b, mlp_w1, mlp_b1, mlp_w2, mlp_b2, ln3_g, ln3_b):
    assert w_in.shape[0] == DEPTH == 1
    nb, t_prompt, _ = x_prompt.shape
    n_dec, t_dec, _ = x_sample.shape
    row = lambda v: v[0].reshape(1, -1)

    wb, cw, pw_re, pw_im = _s5_tables(ssm_a_re[0], ssm_a_im[0], ssm_log_dt[0], ssm_b_re[0], ssm_b_im[0],
                                      ssm_c_re[0], ssm_c_im[0], max(PROMPT_M, t_dec))
    shared = (w_in[0].astype(BF16), wb, cw)
    lane_blocks = lambda a: a.reshape(a.shape[0], -1, LANE).transpose(1, 0, 2)
    rest = (row(ssm_d), glu_w[0].astype(BF16), row(glu_b), lane_blocks(conv_w[0]),
            lane_blocks(row(conv_b)), row(conv_ln_g),
            row(conv_ln_b), w_out[0].astype(BF16), row(ln1_g), row(ln1_b))
    per_row = lambda t, n_steps, S: jnp.repeat(t[:n_steps], S, axis=0).astype(BF16)
    mixer_weights = lambda n_steps, S: shared + (
        pw_re[:n_steps], pw_im[:n_steps], per_row(pw_re, n_steps, S), per_row(pw_im, n_steps, S)) + rest

    post_weights = (mem_w_q[0].astype(BF16), mem_w_o[0].astype(BF16), row(ln2_g), row(ln2_b),
                    mlp_w1[0].astype(BF16), row(mlp_b1), mlp_w2[0].astype(BF16), row(mlp_b2),
                    row(ln3_g), row(ln3_b))

    mk, mv = _memory_kv(mem_prompt, mem_w_k[0].astype(BF16), mem_w_v[0].astype(BF16))
    zero_h = jnp.zeros((nb, 1, N_STATE), F32)
    zero_conv = jnp.zeros((nb, HALO, D_CONV), x_prompt.dtype)
    x1p, p_re, p_im, p_conv = _mixer(x_prompt, zero_h, zero_h, zero_conv, mixer_weights(PROMPT_M, PROMPT_S),
                                     S=PROMPT_S, m=PROMPT_M, chain=True)
    yp = _attn_mlp_prompt(x1p.reshape(nb * t_prompt, D_MODEL), mk, mv, post_weights, t_prompt)
    yp = yp.reshape(nb, t_prompt, D_MODEL)

    cache_sj = cache_conv[0].transpose(1, 0, 2).reshape(1, HALO * n_dec, D_CONV)
    x1s, s_re, s_im, s_conv = _mixer(
        x_sample.reshape(1, n_dec * t_dec, D_MODEL),
        state_ssm_re[0].reshape(1, n_dec, N_STATE), state_ssm_im[0].reshape(1, n_dec, N_STATE),
        cache_sj, mixer_weights(t_dec, n_dec), S=n_dec, m=t_dec, chain=False)
    ys = _attn_mlp_streams(x1s.reshape(n_dec * t_dec, D_MODEL),
                           cache_mem_k[0].reshape(n_dec, N_MEM, D_MODEL),
                           cache_mem_v[0].reshape(n_dec, N_MEM, D_MODEL), post_weights, t_dec)
    ys = ys.reshape(n_dec, t_dec, D_MODEL)
    s_conv = s_conv.reshape(HALO, n_dec, D_CONV).transpose(1, 0, 2)

    st = lambda a, n: a.reshape(1, n, N_GROUPS, SSM_STATE)
    kv5 = lambda a: a.reshape(1, nb, N_MEM, MEM_HEADS, HEAD_DIM)
    return (yp, ys, st(p_re, nb), st(p_im, nb), p_conv.reshape(1, nb, HALO, D_CONV), kv5(mk), kv5(mv),
            st(s_re, n_dec), st(s_im, n_dec), s_conv.reshape(1, n_dec, HALO, D_CONV))
```

```python
import functools
import math

import jax
import jax.numpy as jnp
from jax import lax
from jax.experimental import pallas as pl
from jax.experimental.pallas import tpu as pltpu

F32 = jnp.float32
BF16 = jnp.bfloat16

D_MODEL = 1024
D_SSM = 512
D_CONV = 512
SSM_GROUP = 16
N_GROUPS = 32
SSM_STATE = 64
N_STATE = N_GROUPS * SSM_STATE
CONV_WIDTH = 31
HALO = CONV_WIDTH - 1
N_MEM = 256
MEM_HEADS = 4
HEAD_DIM = 256
D_FF = 4096
D_IN = D_SSM + 2 * D_CONV
LN_EPS = 1e-5
DEPTH = 1
ALPHA = (2.0 * DEPTH) ** 0.25

LANE = 128
SUBLANE = 8
BF16_ROWS = 16
N_TILES = N_STATE // LANE
VMEM_LIMIT = 56 * 1024 * 1024

PROMPT_S = 8
PROMPT_M = 64
ROWS_B = 512
LN_ROWS = 32
CONV_JB = 8
CONV_GROUPS_PER_TRIP = 2
PROJ_CHUNK = 256


def _layer_norm(x, g, b):
    mu = jnp.mean(x, axis=-1, keepdims=True)
    xc = x - mu
    var = jnp.mean(xc * xc, axis=-1, keepdims=True)
    return xc * lax.rsqrt(var + LN_EPS) * g + b


def _gelu_tanh(x):
    return 0.5 * x * (1.0 + jnp.tanh(math.sqrt(2.0 / math.pi) * (x + 0.044715 * (x * x * x))))


def _pieces(n_rows, chunk):
    chunk = min(chunk, n_rows)
    return [slice(r0, r0 + chunk) for r0 in range(0, n_rows, chunk)]


def _bdot(a, b):
    return jnp.dot(a, b, preferred_element_type=F32)


def _kv_kernel(mem_ref, wk_ref, wv_ref, k_ref, v_ref):
    mb = mem_ref[...].astype(BF16)
    k_ref[...] = _bdot(mb, wk_ref[...])
    v_ref[...] = _bdot(mb, wv_ref[...])


def _memory_kv(mem, wk, wv):
    nb = mem.shape[0]
    wspec = pl.BlockSpec((D_MODEL, D_MODEL), lambda b: (0, 0), pipeline_mode=pl.Buffered(1))
    ospec = pl.BlockSpec((None, N_MEM, D_MODEL), lambda b: (b, 0, 0))
    return pl.pallas_call(
        _kv_kernel,
        grid=(nb,),
        in_specs=[pl.BlockSpec((None, N_MEM, D_MODEL), lambda b: (b, 0, 0)), wspec, wspec],
        out_specs=[ospec, ospec],
        out_shape=[jax.ShapeDtypeStruct((nb, N_MEM, D_MODEL), F32)] * 2,
        compiler_params=pltpu.CompilerParams(dimension_semantics=("arbitrary",),
                                             vmem_limit_bytes=VMEM_LIMIT),
        name="memory_kv",
    )(mem, wk, wv)


def _mixer_kernel(x_ref, xn_ref, h0r_ref, h0i_ref, cache_ref, perm_ref, permt_ref, w_in_ref, wb_ref, cw_ref,
                  pwr_ref, pwi_ref, pwbr_ref, pwbi_ref, d_ref, gluw_ref, glub_ref, convw_ref, convb_ref, clng_ref, clnb_ref,
                  wout_ref, ln1g_ref, ln1b_ref,
                  x1_ref, hr_out, hi_out, conv_out,
                  xb, xbn, xpn, proj, ub, bu, hb, vbuf, tail, mixb, ire, iim, car_r, car_i, vblk, cbuf,
                  *, S, m, chain, blocks_per_stream):
    TB = S * m
    g = pl.program_id(0)
    i = g % blocks_per_stream
    last = blocks_per_stream - 1
    slot = g % 2
    n_sub = S // SUBLANE
    n_xc = D_MODEL // PROJ_CHUNK
    n_pc = D_IN // PROJ_CHUNK

    def pget(rows, col0, ncols):
        out, c = [], col0
        while c < col0 + ncols:
            ch, off = divmod(c, PROJ_CHUNK)
            w = min(PROJ_CHUNK - off, col0 + ncols - c)
            out.append(proj[slot, ch, rows, off:off + w])
            c += w
        return out[0] if len(out) == 1 else jnp.concatenate(out, axis=1)

    def pset(rows, col0, val):
        c = 0
        while c < val.shape[1]:
            ch, off = divmod(col0 + c, PROJ_CHUNK)
            w = min(PROJ_CHUNK - off, val.shape[1] - c)
            proj[slot, ch, rows, off:off + w] = val[:, c:c + w]
            c += w

    def cast_chunks(src):
        for rows in _pieces(TB, 64):
            for c in range(n_xc):
                xbn[c, rows, :] = src[rows, c * PROJ_CHUNK:(c + 1) * PROJ_CHUNK].astype(BF16)

    def interleave_chunk(c):
        xpn[c] = _bdot(perm_ref[...], xbn[c]).astype(BF16)

    def project_chunk(n, dst_slot):
        acc = None
        for k in range(n_xc):
            t = _bdot(xpn[k], w_in_ref[n, k])
            acc = t if acc is None else acc + t
        proj[dst_slot, n] = acc

    @pl.when(g == 0)
    def _():
        cast_chunks(x_ref)
        for c in range(n_xc):
            interleave_chunk(c)
        for n in range(n_pc):
            project_chunk(n, 0)

    cast_chunks(xn_ref)

    if chain:
        @pl.when(i == 0)
        def _():
            for k in range(HALO):
                tail[k * S:(k + 1) * S, :] = jnp.broadcast_to(cache_ref[k:k + 1, :], (S, D_CONV))
            ire[0:1, :] = h0r_ref[...]
            iim[0:1, :] = h0i_ref[...]

        @pl.when(i > 0)
        def _():
            tail[...] = vbuf[m * S:(m + HALO) * S, :]
            ire[0:1, :] = car_r[...]
            iim[0:1, :] = car_i[...]

    for rows in _pieces(TB, 64):
        for kb in range(D_SSM // LANE):
            ub[kb, rows, :] = pget(rows, kb * LANE, LANE).astype(BF16)
        vbuf[HALO * S + rows.start:HALO * S + rows.stop, :] = (
            pget(rows, D_SSM, D_CONV) * jax.nn.sigmoid(pget(rows, D_SSM + D_CONV, D_CONV)))

    if chain:
        sub = lax.broadcasted_iota(jnp.int32, (S, D_CONV), 0)
        for k in range(HALO):
            cur = vbuf[(m + k) * S:(m + k + 1) * S, :]
            prev = tail[k * S:(k + 1) * S, :]
            vbuf[k * S:(k + 1) * S, :] = pltpu.roll(jnp.where(sub == S - 1, prev, cur), 1, 0)
    else:
        vbuf[0:HALO * S, :] = cache_ref[...]
    for rows in _pieces((m + HALO) * S, 64):
        for lb in range(D_CONV // LANE):
            vblk[lb, rows, :] = vbuf[rows, lb * LANE:(lb + 1) * LANE]

    jb = CONV_JB
    per_lb = n_sub * (m // jb)
    trips_per_lb = per_lb // CONV_GROUPS_PER_TRIP
    n_trips = trips_per_lb * (D_CONV // LANE)
    tiles_per_trip = N_TILES // n_trips
    assert trips_per_lb * CONV_GROUPS_PER_TRIP == per_lb and tiles_per_trip * n_trips == N_TILES

    def conv_trip(i, c, head_partner):
        lb = i // trips_per_lb
        wv = [jnp.broadcast_to(convw_ref[lb, k:k + 1, :], (SUBLANE, LANE)) for k in range(CONV_WIDTH)]
        bias = jnp.broadcast_to(convb_ref[lb], (SUBLANE, LANE))
        for gq in range(CONV_GROUPS_PER_TRIP):
            r = (i % trips_per_lb) * CONV_GROUPS_PER_TRIP + gq
            base = pl.multiple_of((r % (m // jb)) * (jb * S) + (r // (m // jb)) * SUBLANE, SUBLANE)
            accs = [bias] * jb
            for t in range(jb + HALO):
                row = vblk[lb, pl.ds(base + t * S, SUBLANE), :]
                for q in range(jb):
                    k = t - q
                    if 0 <= k < CONV_WIDTH:
                        accs[q] = accs[q] + wv[k] * row
            for q in range(jb):
                cbuf[lb, pl.ds(base + q * S, SUBLANE), :] = accs[q]
        for tq in range(tiles_per_trip):
            tile = i * tiles_per_trip + tq
            res = _bdot(ub[tile // 4], wb_ref[tile])
            bu[tile] = res[:, 0:LANE]
            bu[N_TILES + tile] = res[:, LANE:2 * LANE]
        head_partner(i)
        return c

    n_pairs = n_pc // 2
    n_perm_trips = n_trips // 4
    chunks_per_trip = n_xc // n_perm_trips
    trips_per_pair = (n_trips - n_perm_trips) // n_pairs
    assert chunks_per_trip * n_perm_trips == n_xc and n_perm_trips + trips_per_pair * n_pairs == n_trips

    def interleave_partner(t):
        for q in range(chunks_per_trip):
            interleave_chunk(t * chunks_per_trip + q)

    lax.fori_loop(0, n_perm_trips, functools.partial(conv_trip, head_partner=interleave_partner), 0,
                  unroll=2)

    def pair_body(p, c):
        for q in range(trips_per_pair):
            conv_trip(n_perm_trips + p * trips_per_pair + q, c, head_partner=lambda t: None)
        project_chunk(2 * p, 1 - slot)
        project_chunk(2 * p + 1, 1 - slot)
        return c
    lax.fori_loop(0, n_pairs, pair_body, 0)

    def bu_get(rows, t0, nt, part):
        return jnp.concatenate([bu[part * N_TILES + t, rows, :] for t in range(t0, t0 + nt)], axis=1)

    def bu_set(rows, t0, part, val):
        for idx in range(val.shape[1] // LANE):
            bu[part * N_TILES + t0 + idx, rows, :] = val[:, idx * LANE:(idx + 1) * LANE]

    lch = (8 * 1024) // S
    nt = lch // LANE
    for lc in range(N_STATE // lch):
        l0 = lc * lch
        ar = jnp.broadcast_to(pwr_ref[0:1, l0:l0 + lch], (S, lch))
        ai = jnp.broadcast_to(pwi_ref[0:1, l0:l0 + lch], (S, lch))

        def scan_body(j, carry, t0=lc * nt, ar=ar, ai=ai):
            hr, hi = carry
            rows = pl.ds(pl.multiple_of(j * S, S), S)
            nr = ar * hr - ai * hi + bu_get(rows, t0, nt, 0)
            ni = ar * hi + ai * hr + bu_get(rows, t0, nt, 1)
            bu_set(rows, t0, 0, nr)
            bu_set(rows, t0, 1, ni)
            return nr, ni
        z = jnp.zeros((S, lch), F32)
        lax.fori_loop(0, m, scan_body, (z, z), unroll=2)

    if chain:
        amr = pwr_ref[m - 1:m, :]
        ami = pwi_ref[m - 1:m, :]
        cr = ire[0:1, :]
        ci = iim[0:1, :]
        for s in range(S):
            row = (m - 1) * S + s
            zr = bu_get(slice(row, row + 1), 0, N_TILES, 0)
            zi = bu_get(slice(row, row + 1), 0, N_TILES, 1)
            cr, ci = amr * cr - ami * ci + zr, amr * ci + ami * cr + zi
            if s < S - 1:
                ire[s + 1:s + 2, :] = cr
                iim[s + 1:s + 2, :] = ci
        car_r[...] = cr
        car_i[...] = ci

        @pl.when(i == last)
        def _():
            hr_out[...] = cr
            hi_out[...] = ci
    in_r = ire if chain else h0r_ref
    in_i = iim if chain else h0i_ref

    reps = BF16_ROWS // S if S < BF16_ROWS else 1
    tile_rows = reps * S
    for lc in range(N_STATE // 512):
        l0 = lc * 512
        e_r = jnp.concatenate([in_r[:, l0:l0 + 512]] * reps, axis=0).astype(BF16)
        e_i = jnp.concatenate([in_i[:, l0:l0 + 512]] * reps, axis=0).astype(BF16)

        def fix_body(jp, c, l0=l0, e_r=e_r, e_i=e_i):
            rows = pl.ds(pl.multiple_of(jp * tile_rows, tile_rows), tile_rows)
            pr = pwbr_ref[rows, l0:l0 + 512]
            pi = pwbi_ref[rows, l0:l0 + 512]
            hz_r = bu_get(rows, l0 // LANE, 512 // LANE, 0).astype(BF16)
            hz_i = bu_get(rows, l0 // LANE, 512 // LANE, 1).astype(BF16)
            hb[rows, l0:l0 + 512] = hz_r + (pr * e_r - pi * e_i)
            hb[rows, N_STATE + l0:N_STATE + l0 + 512] = hz_i + (pr * e_i + pi * e_r)
            return c
        lax.fori_loop(0, TB // tile_rows, fix_body, 0, unroll=2)

    if not chain:
        rows = slice((m - 1) * S, m * S)
        pr = pwr_ref[m - 1:m, :]
        pi = pwi_ref[m - 1:m, :]
        e_r = in_r[...]
        e_i = in_i[...]
        hr_out[...] = bu_get(rows, 0, N_TILES, 0) + pr * e_r - pi * e_i
        hi_out[...] = bu_get(rows, 0, N_TILES, 1) + pr * e_i + pi * e_r

    y0 = D_SSM + D_CONV
    for ob in range(D_SSM // LANE):
        acc = None
        for c in range(4 * ob, 4 * ob + 4):
            lhs = jnp.concatenate([hb[:, c * LANE:(c + 1) * LANE],
                                   hb[:, N_STATE + c * LANE:N_STATE + (c + 1) * LANE]], axis=1)
            t = _bdot(lhs, cw_ref[c])
            acc = t if acc is None else acc + t
        pset(slice(None), y0 + ob * LANE, acc)

    for rows in _pieces(TB, 64):
        z = _gelu_tanh(pget(rows, y0, D_SSM) + d_ref[...] * pget(rows, 0, D_SSM))
        pset(rows, y0, z)
        mixb[rows, 0:D_SSM] = z.astype(BF16)
    for n in range(D_SSM // PROJ_CHUNK):
        cols = slice(n * PROJ_CHUNK, (n + 1) * PROJ_CHUNK)
        pset(slice(None), D_SSM + n * PROJ_CHUNK, _bdot(mixb[:, 0:D_SSM], gluw_ref[:, cols]))
    for rows in _pieces(TB, 64):
        gate = jax.nn.sigmoid(pget(rows, D_SSM, D_CONV) + glub_ref[...])
        mixb[rows, 0:D_SSM] = (pget(rows, y0, D_SSM) * gate).astype(BF16)

    for rows in _pieces(TB, 64):
        conv = jnp.concatenate([cbuf[lb, rows, :] for lb in range(D_CONV // LANE)], axis=1)
        h = _layer_norm(conv, clng_ref[...], clnb_ref[...])
        mixb[rows, D_SSM:D_MODEL] = (h * jax.nn.sigmoid(h)).astype(BF16)

    if chain:
        @pl.when(i == last)
        def _():
            for k in range(HALO):
                row = (m + k) * S + S - 1
                conv_out[k:k + 1, :] = vbuf[row:row + 1, :]
    else:
        conv_out[...] = vbuf[m * S:(m + HALO) * S, :]

    xb[...] = _bdot(permt_ref[...], mixb[...]).astype(BF16)
    for n in range(n_xc):
        cols = slice(n * PROJ_CHUNK, (n + 1) * PROJ_CHUNK)
        pset(slice(None), n * PROJ_CHUNK, _bdot(xb[...], wout_ref[:, cols]))
    for rows in _pieces(TB, LN_ROWS):
        x1_ref[rows, :] = _layer_norm(ALPHA * x_ref[rows, :] + pget(rows, 0, D_MODEL),
                                      ln1g_ref[...], ln1b_ref[...])


def _const_spec(shape):
    nd = len(shape)
    return pl.BlockSpec(shape, lambda g: (0,) * nd, pipeline_mode=pl.Buffered(1))


def _interleave_perm(S, m):
    r = jnp.arange(S * m)
    p = jax.nn.one_hot((r % S) * m + r // S, S * m, dtype=BF16)
    return p, p.T


def _mixer(x, h0r, h0i, cache, wts, *, S, m, chain):
    nb, T, _ = x.shape
    TB = S * m
    nblk = T // TB
    assert T == nblk * TB and S % SUBLANE == 0 and m % CONV_JB == 0
    assert (not chain) or (S == SUBLANE and m >= HALO)
    s0 = h0r.shape[1]
    crow = cache.shape[1]
    n_blocks = nb * nblk
    nxt = lambda g: jnp.minimum(g + 1, n_blocks - 1)
    bspec = lambda shape: pl.BlockSpec((None,) + shape, lambda g: (g // nblk, 0, 0))
    xspec = pl.BlockSpec((None, TB, D_MODEL), lambda g: (g // nblk, g % nblk, 0))
    xnspec = pl.BlockSpec((None, TB, D_MODEL), lambda g: (nxt(g) // nblk, nxt(g) % nblk, 0))
    wts = _interleave_perm(S, m) + tuple(wts)
    in_specs = [xspec, xnspec, bspec((s0, N_STATE)), bspec((s0, N_STATE)), bspec((crow, D_CONV))]
    in_specs += [_const_spec(w.shape) for w in wts]
    out_specs = [xspec, bspec((s0, N_STATE)), bspec((s0, N_STATE)), bspec((crow, D_CONV))]
    out_shape = [jax.ShapeDtypeStruct((nb, T, D_MODEL), F32),
                 jax.ShapeDtypeStruct((nb, s0, N_STATE), F32),
                 jax.ShapeDtypeStruct((nb, s0, N_STATE), F32),
                 jax.ShapeDtypeStruct((nb, crow, D_CONV), F32)]
    scratch = [pltpu.VMEM((TB, D_MODEL), BF16),
               pltpu.VMEM((D_MODEL // PROJ_CHUNK, TB, PROJ_CHUNK), BF16),
               pltpu.VMEM((D_MODEL // PROJ_CHUNK, TB, PROJ_CHUNK), BF16),
               pltpu.VMEM((2, D_IN // PROJ_CHUNK, TB, PROJ_CHUNK), F32),
               pltpu.VMEM((D_SSM // LANE, TB, LANE), BF16),
               pltpu.VMEM((2 * N_TILES, TB, LANE), F32),
               pltpu.VMEM((TB, 2 * N_STATE), BF16),
               pltpu.VMEM(((m + HALO) * S, D_CONV), F32),
               pltpu.VMEM((HALO * S, D_CONV), F32),
               pltpu.VMEM((TB, D_MODEL), BF16),
               pltpu.VMEM((S, N_STATE), F32),
               pltpu.VMEM((S, N_STATE), F32),
               pltpu.VMEM((1, N_STATE), F32),
               pltpu.VMEM((1, N_STATE), F32),
               pltpu.VMEM((D_CONV // LANE, (m + HALO) * S, LANE), F32),
               pltpu.VMEM((D_CONV // LANE, TB, LANE), F32)]
    return pl.pallas_call(
        functools.partial(_mixer_kernel, S=S, m=m, chain=chain, blocks_per_stream=nblk),
        grid=(n_blocks,),
        in_specs=in_specs,
        out_specs=out_specs,
        out_shape=out_shape,
        scratch_shapes=scratch,
        compiler_params=pltpu.CompilerParams(dimension_semantics=("arbitrary",),
                                             vmem_limit_bytes=VMEM_LIMIT),
        name="mixer_chain" if chain else "mixer_streams",
    )(x, x, h0r, h0i, cache, *wts)


def _attend(q, kb, vb):
    outs = []
    for h in range(MEM_HEADS):
        cols = slice(h * HEAD_DIM, (h + 1) * HEAD_DIM)
        s = lax.dot_general(q[:, cols], kb[:, cols], (((1,), (1,)), ((), ())),
                            preferred_element_type=F32) * (HEAD_DIM ** -0.5)
        e = jnp.exp(s - jnp.max(s, axis=-1, keepdims=True))
        p = (e / jnp.sum(e, axis=-1, keepdims=True)).astype(BF16)
        outs.append(_bdot(p, vb[:, cols]).astype(BF16))
    return jnp.concatenate(outs, axis=1)


def _post_attention(halves, x1_ref, ob, wo_ref, ln2g, ln2b, w1_ref, b1_ref, w2_ref, b2_ref, ln3g, ln3b,
                    out_ref, fbuf, x2buf, xb, hidb):
    for hs in halves:
        fbuf[hs, :] = _bdot(ob[hs, :], wo_ref[...])
    for hs in halves:
        for rows in _pieces(hs.stop - hs.start, LN_ROWS):
            rows = slice(hs.start + rows.start, hs.start + rows.stop)
            x2 = _layer_norm(ALPHA * x1_ref[rows, :] + fbuf[rows, :], ln2g[...], ln2b[...])
            x2buf[rows, :] = x2
            xb[rows, :] = x2.astype(BF16)
        for n in range(D_FF // D_MODEL):
            cols = slice(n * D_MODEL, (n + 1) * D_MODEL)
            a = jnp.maximum(_bdot(xb[hs, :], w1_ref[:, cols]) + b1_ref[:, cols], 0.0)
            hidb[hs, cols] = (a * a).astype(BF16)
    for hs in halves:
        fbuf[hs, :] = _bdot(hidb[hs, :], w2_ref[...])
        for rows in _pieces(hs.stop - hs.start, LN_ROWS):
            rows = slice(hs.start + rows.start, hs.start + rows.stop)
            out_ref[rows, :] = _layer_norm(ALPHA * x2buf[rows, :] + fbuf[rows, :] + b2_ref[...],
                                           ln3g[...], ln3b[...])


def _attn_mlp_prompt_kernel(x1_ref, k_ref, v_ref, wq_ref, wo_ref, ln2g, ln2b, w1_ref, b1_ref,
                            w2_ref, b2_ref, ln3g, ln3b, out_ref,
                            xb, qb, ob, kb, vb, fbuf, x2buf, hidb, *, blocks_per_stream):
    n_rows = x1_ref.shape[0]
    halves = _pieces(n_rows, n_rows // 2)

    @pl.when(pl.program_id(0) % blocks_per_stream == 0)
    def _():
        kb[...] = k_ref[...].astype(BF16)
        vb[...] = v_ref[...].astype(BF16)

    for rows in _pieces(n_rows, 64):
        xb[rows, :] = x1_ref[rows, :].astype(BF16)
    for hs in halves:
        qb[hs, :] = _bdot(xb[hs, :], wq_ref[...]).astype(BF16)
    for hs in halves:
        ob[hs, :] = _attend(qb[hs, :], kb, vb)
    _post_attention(halves, x1_ref, ob, wo_ref, ln2g, ln2b, w1_ref, b1_ref, w2_ref, b2_ref, ln3g, ln3b,
                    out_ref, fbuf, x2buf, xb, hidb)


def _attn_mlp_streams_kernel(x1_ref, k_ref, v_ref, wq_ref, wo_ref, ln2g, ln2b, w1_ref, b1_ref,
                             w2_ref, b2_ref, ln3g, ln3b, out_ref,
                             xb, qb, ob, kb, vb, fbuf, x2buf, hidb, *, t_len):
    n_rows = x1_ref.shape[0]
    s_id = pl.program_id(0)

    @pl.when(s_id == 0)
    def _():
        xb[...] = x1_ref[...].astype(BF16)
        qb[...] = _bdot(xb[...], wq_ref[...]).astype(BF16)

    kb[...] = k_ref[...].astype(BF16)
    vb[...] = v_ref[...].astype(BF16)
    rows = pl.ds(pl.multiple_of(s_id * t_len, t_len), t_len)
    ob[rows, :] = _attend(qb[rows, :], kb, vb)

    @pl.when(s_id == pl.num_programs(0) - 1)
    def _():
        _post_attention(_pieces(n_rows, n_rows), x1_ref, ob, wo_ref, ln2g, ln2b, w1_ref, b1_ref, w2_ref,
                        b2_ref, ln3g, ln3b, out_ref, fbuf, x2buf, xb, hidb)


def _attn_mlp_scratch(rows):
    return [pltpu.VMEM((rows, D_MODEL), BF16),
            pltpu.VMEM((rows, D_MODEL), BF16),
            pltpu.VMEM((rows, D_MODEL), BF16),
            pltpu.VMEM((N_MEM, D_MODEL), BF16),
            pltpu.VMEM((N_MEM, D_MODEL), BF16),
            pltpu.VMEM((rows, D_MODEL), F32),
            pltpu.VMEM((rows, D_MODEL), F32),
            pltpu.VMEM((rows, D_FF), BF16)]


def _w1(shape):
    nd = len(shape)
    return pl.BlockSpec(shape, lambda i: (0,) * nd, pipeline_mode=pl.Buffered(1))


def _attn_mlp_prompt(x1, mem_k, mem_v, wts, rows_per_stream):
    n = x1.shape[0]
    blk_per_stream = rows_per_stream // ROWS_B
    assert rows_per_stream % ROWS_B == 0
    kvspec = pl.BlockSpec((None, N_MEM, D_MODEL), lambda i: (i // blk_per_stream, 0, 0))
    xspec = pl.BlockSpec((ROWS_B, D_MODEL), lambda i: (i, 0))
    return pl.pallas_call(
        functools.partial(_attn_mlp_prompt_kernel, blocks_per_stream=blk_per_stream),
        grid=(n // ROWS_B,),
        in_specs=[xspec, kvspec, kvspec] + [_w1(w.shape) for w in wts],
        out_specs=xspec,
        out_shape=jax.ShapeDtypeStruct((n, D_MODEL), F32),
        scratch_shapes=_attn_mlp_scratch(ROWS_B),
        compiler_params=pltpu.CompilerParams(dimension_semantics=("arbitrary",),
                                             vmem_limit_bytes=VMEM_LIMIT),
        name="attn_mlp_prompt",
    )(x1, mem_k, mem_v, *wts)


def _attn_mlp_streams(x1, mem_k, mem_v, wts, t_len):
    n = x1.shape[0]
    n_streams = n // t_len
    kvspec = pl.BlockSpec((None, N_MEM, D_MODEL), lambda s: (s, 0, 0))
    xspec = pl.BlockSpec((n, D_MODEL), lambda s: (0, 0))
    return pl.pallas_call(
        functools.partial(_attn_mlp_streams_kernel, t_len=t_len),
        grid=(n_streams,),
        in_specs=[xspec, kvspec, kvspec] + [_w1(w.shape) for w in wts],
        out_specs=xspec,
        out_shape=jax.ShapeDtypeStruct((n, D_MODEL), F32),
        scratch_shapes=_attn_mlp_scratch(n),
        compiler_params=pltpu.CompilerParams(dimension_semantics=("arbitrary",),
                                             vmem_limit_bytes=VMEM_LIMIT),
        name="attn_mlp_streams",
    )(x1, mem_k, mem_v, *wts)


def _s5_tables(a_re, a_im, log_dt, b_re, b_im, c_re, c_im, n_steps):
    dt = jnp.exp(log_dt)[:, None]
    mag = jnp.exp(a_re * dt)
    ang = a_im * dt
    ab_re = mag * jnp.cos(ang)
    ab_im = mag * jnp.sin(ang)
    den = a_re * a_re + a_im * a_im
    p = ab_re - 1.0
    q = ab_im
    k_re = ((p * a_re + q * a_im) / den)[..., None]
    k_im = ((q * a_re - p * a_im) / den)[..., None]
    bb_re = k_re * b_re - k_im * b_im
    bb_im = k_re * b_im + k_im * b_re

    steps = jnp.arange(1, n_steps + 1, dtype=F32)[:, None, None]
    pmag = jnp.exp(a_re * dt * steps)
    pang = a_im * dt * steps
    pw_re = (pmag * jnp.cos(pang)).reshape(n_steps, N_STATE)
    pw_im = (pmag * jnp.sin(pang)).reshape(n_steps, N_STATE)
    pw_re = pw_re.at[0].set(ab_re.reshape(N_STATE))
    pw_im = pw_im.at[0].set(ab_im.reshape(N_STATE))

    slot = jax.nn.one_hot(jnp.arange(N_GROUPS).reshape(N_TILES, 2) % 8, 8, dtype=F32)
    place_b = lambda b: jnp.einsum("cqnp,cqr->crpqn", b.reshape(N_TILES, 2, SSM_STATE, SSM_GROUP),
                                   slot).reshape(N_TILES, LANE, LANE)
    wb = jnp.concatenate([place_b(bb_re), place_b(bb_im)], axis=2).astype(BF16)
    place_c = lambda cm: jnp.einsum("cqpn,cqr->cqnrp", cm.reshape(N_TILES, 2, SSM_GROUP, SSM_STATE),
                                    slot).reshape(N_TILES, LANE, LANE)
    cw = jnp.concatenate([place_c(c_re), -place_c(c_im)], axis=1).astype(BF16)
    return wb, cw, pw_re, pw_im


def kernel(x_prompt, x_sample, state_ssm_re, state_ssm_im, cache_conv, cache_mem_k, cache_mem_v, mem_prompt, w_in, ssm_a_re, ssm_a_im, ssm_log_dt, ssm_b_re, ssm_b_im, ssm_c_re, ssm_c_im, ssm_d, glu_w, glu_b, conv_w, conv_b, conv_ln_g, conv_ln_b, w_out, ln1_g, ln1_b, mem_w_q, mem_w_k, mem_w_v, mem_w_o, ln2_g, ln2_b, mlp_w1, mlp_b1, mlp_w2, mlp_b2, ln3_g, ln3_b):
    assert w_in.shape[0] == DEPTH == 1
    nb, t_prompt, _ = x_prompt.shape
    n_dec, t_dec, _ = x_sample.shape
    row = lambda v: v[0].reshape(1, -1)

    wb, cw, pw_re, pw_im = _s5_tables(ssm_a_re[0], ssm_a_im[0], ssm_log_dt[0], ssm_b_re[0], ssm_b_im[0],
                                      ssm_c_re[0], ssm_c_im[0], max(PROMPT_M, t_dec))
    w_in_tiles = w_in[0].astype(BF16).reshape(D_MODEL // PROJ_CHUNK, PROJ_CHUNK, D_IN // PROJ_CHUNK,
                                              PROJ_CHUNK).transpose(2, 0, 1, 3)
    shared = (w_in_tiles, wb, cw)
    lane_blocks = lambda a: a.reshape(a.shape[0], -1, LANE).transpose(1, 0, 2)
    rest = (row(ssm_d), glu_w[0].astype(BF16), row(glu_b), lane_blocks(conv_w[0]),
            lane_blocks(row(conv_b)), row(conv_ln_g),
            row(conv_ln_b), w_out[0].astype(BF16), row(ln1_g), row(ln1_b))
    per_row = lambda t, n_steps, S: jnp.repeat(t[:n_steps], S, axis=0).astype(BF16)
    mixer_weights = lambda n_steps, S: shared + (
        pw_re[:n_steps], pw_im[:n_steps], per_row(pw_re, n_steps, S), per_row(pw_im, n_steps, S)) + rest

    post_weights = (mem_w_q[0].astype(BF16), mem_w_o[0].astype(BF16), row(ln2_g), row(ln2_b),
                    mlp_w1[0].astype(BF16), row(mlp_b1), mlp_w2[0].astype(BF16), row(mlp_b2),
                    row(ln3_g), row(ln3_b))

    mk, mv = _memory_kv(mem_prompt, mem_w_k[0].astype(BF16), mem_w_v[0].astype(BF16))
    zero_h = jnp.zeros((nb, 1, N_STATE), F32)
    zero_conv = jnp.zeros((nb, HALO, D_CONV), x_prompt.dtype)
    x1p, p_re, p_im, p_conv = _mixer(x_prompt, zero_h, zero_h, zero_conv, mixer_weights(PROMPT_M, PROMPT_S),
                                     S=PROMPT_S, m=PROMPT_M, chain=True)
    yp = _attn_mlp_prompt(x1p.reshape(nb * t_prompt, D_MODEL), mk, mv, post_weights, t_prompt)
    yp = yp.reshape(nb, t_prompt, D_MODEL)

    cache_sj = cache_conv[0].transpose(1, 0, 2).reshape(1, HALO * n_dec, D_CONV)
    x1s, s_re, s_im, s_conv = _mixer(
        x_sample.reshape(1, n_dec * t_dec, D_MODEL),
        state_ssm_re[0].reshape(1, n_dec, N_STATE), state_ssm_im[0].reshape(1, n_dec, N_STATE),
        cache_sj, mixer_weights(t_dec, n_dec), S=n_dec, m=t_dec, chain=False)
    ys = _attn_mlp_streams(x1s.reshape(n_dec * t_dec, D_MODEL),
                           cache_mem_k[0].reshape(n_dec, N_MEM, D_MODEL),
                           cache_mem_v[0].reshape(n_dec, N_MEM, D_MODEL), post_weights, t_dec)
    ys = ys.reshape(n_dec, t_dec, D_MODEL)
    s_conv = s_conv.reshape(HALO, n_dec, D_CONV).transpose(1, 0, 2)

    st = lambda a, n: a.reshape(1, n, N_GROUPS, SSM_STATE)
    kv5 = lambda a: a.reshape(1, nb, N_MEM, MEM_HEADS, HEAD_DIM)
    return (yp, ys, st(p_re, nb), st(p_im, nb), p_conv.reshape(1, nb, HALO, D_CONV), kv5(mk), kv5(mv),
            st(s_re, n_dec), st(s_im, n_dec), s_conv.reshape(1, n_dec, HALO, D_CONV))
```

```python
import functools
import math

import jax
import jax.numpy as jnp
from jax import lax
from jax.experimental import pallas as pl
from jax.experimental.pallas import tpu as pltpu

F32 = jnp.float32
BF16 = jnp.bfloat16

D_MODEL = 1024
D_SSM = 512
D_CONV = 512
SSM_GROUP = 16
N_GROUPS = 32
SSM_STATE = 64
N_STATE = N_GROUPS * SSM_STATE
CONV_WIDTH = 31
HALO = CONV_WIDTH - 1
N_MEM = 256
MEM_HEADS = 4
HEAD_DIM = 256
D_FF = 4096
D_IN = D_SSM + 2 * D_CONV
LN_EPS = 1e-5
DEPTH = 1
ALPHA = (2.0 * DEPTH) ** 0.25

LANE = 128
SUBLANE = 8
BF16_ROWS = 16
N_TILES = N_STATE // LANE
VMEM_LIMIT = 56 * 1024 * 1024

PROMPT_S = 8
PROMPT_M = 64
ROWS_B = 512
LN_ROWS = 32
CONV_JB = 8
CONV_GROUPS_PER_TRIP = 2
PROJ_CHUNK = 256


def _layer_norm(x, g, b):
    mu = jnp.mean(x, axis=-1, keepdims=True)
    xc = x - mu
    var = jnp.mean(xc * xc, axis=-1, keepdims=True)
    return xc * lax.rsqrt(var + LN_EPS) * g + b


def _gelu_tanh(x):
    return 0.5 * x * (1.0 + jnp.tanh(math.sqrt(2.0 / math.pi) * (x + 0.044715 * (x * x * x))))


def _pieces(n_rows, chunk):
    chunk = min(chunk, n_rows)
    return [slice(r0, r0 + chunk) for r0 in range(0, n_rows, chunk)]


def _bdot(a, b):
    return jnp.dot(a, b, preferred_element_type=F32)


def _kv_kernel(mem_ref, wk_ref, wv_ref, k_ref, v_ref):
    mb = mem_ref[...].astype(BF16)
    k_ref[...] = _bdot(mb, wk_ref[...])
    v_ref[...] = _bdot(mb, wv_ref[...])


def _memory_kv(mem, wk, wv):
    nb = mem.shape[0]
    wspec = pl.BlockSpec((D_MODEL, D_MODEL), lambda b: (0, 0), pipeline_mode=pl.Buffered(1))
    ospec = pl.BlockSpec((None, N_MEM, D_MODEL), lambda b: (b, 0, 0))
    return pl.pallas_call(
        _kv_kernel,
        grid=(nb,),
        in_specs=[pl.BlockSpec((None, N_MEM, D_MODEL), lambda b: (b, 0, 0)), wspec, wspec],
        out_specs=[ospec, ospec],
        out_shape=[jax.ShapeDtypeStruct((nb, N_MEM, D_MODEL), F32)] * 2,
        compiler_params=pltpu.CompilerParams(dimension_semantics=("arbitrary",),
                                             vmem_limit_bytes=VMEM_LIMIT),
        name="memory_kv",
    )(mem, wk, wv)


def _mixer_kernel(x_ref, xn_ref, h0r_ref, h0i_ref, cache_ref, perm_ref, permt_ref, w_in_ref, wb_ref, cw_ref,
                  pwr_ref, pwi_ref, pwbr_ref, pwbi_ref, d_ref, gluw_ref, glub_ref, convw_ref, convb_ref, clng_ref, clnb_ref,
                  wout_ref,
                  mix_ref, hr_out, hi_out, conv_out,
                  xb, xbn, xpn, proj, ub, bu, hb, vbuf, tail, mixb, ire, iim, car_r, car_i, vblk, cbuf,
                  *, S, m, chain, blocks_per_stream):
    TB = S * m
    g = pl.program_id(0)
    i = g % blocks_per_stream
    last = blocks_per_stream - 1
    slot = g % 2
    n_sub = S // SUBLANE
    n_xc = D_MODEL // PROJ_CHUNK
    n_pc = D_IN // PROJ_CHUNK

    def pget(rows, col0, ncols):
        out, c = [], col0
        while c < col0 + ncols:
            ch, off = divmod(c, PROJ_CHUNK)
            w = min(PROJ_CHUNK - off, col0 + ncols - c)
            out.append(proj[slot, ch, rows, off:off + w])
            c += w
        return out[0] if len(out) == 1 else jnp.concatenate(out, axis=1)

    def pset(rows, col0, val):
        c = 0
        while c < val.shape[1]:
            ch, off = divmod(col0 + c, PROJ_CHUNK)
            w = min(PROJ_CHUNK - off, val.shape[1] - c)
            proj[slot, ch, rows, off:off + w] = val[:, c:c + w]
            c += w

    def cast_chunks(src):
        for rows in _pieces(TB, 64):
            for c in range(n_xc):
                xbn[c, rows, :] = src[rows, c * PROJ_CHUNK:(c + 1) * PROJ_CHUNK].astype(BF16)

    def interleave_chunk(c):
        xpn[c] = _bdot(perm_ref[...], xbn[c]).astype(BF16)

    def project_chunk(n, dst_slot):
        acc = None
        for k in range(n_xc):
            t = _bdot(xpn[k], w_in_ref[n, k])
            acc = t if acc is None else acc + t
        proj[dst_slot, n] = acc

    @pl.when(g == 0)
    def _():
        cast_chunks(x_ref)
        for c in range(n_xc):
            interleave_chunk(c)
        for n in range(n_pc):
            project_chunk(n, 0)

    cast_chunks(xn_ref)

    if chain:
        @pl.when(i == 0)
        def _():
            for k in range(HALO):
                tail[k * S:(k + 1) * S, :] = jnp.broadcast_to(cache_ref[k:k + 1, :], (S, D_CONV))
            ire[0:1, :] = h0r_ref[...]
            iim[0:1, :] = h0i_ref[...]

        @pl.when(i > 0)
        def _():
            tail[...] = vbuf[m * S:(m + HALO) * S, :]
            ire[0:1, :] = car_r[...]
            iim[0:1, :] = car_i[...]

    for rows in _pieces(TB, 64):
        for kb in range(D_SSM // LANE):
            ub[kb, rows, :] = pget(rows, kb * LANE, LANE).astype(BF16)
        vbuf[HALO * S + rows.start:HALO * S + rows.stop, :] = (
            pget(rows, D_SSM, D_CONV) * jax.nn.sigmoid(pget(rows, D_SSM + D_CONV, D_CONV)))

    if chain:
        sub = lax.broadcasted_iota(jnp.int32, (S, D_CONV), 0)
        for k in range(HALO):
            cur = vbuf[(m + k) * S:(m + k + 1) * S, :]
            prev = tail[k * S:(k + 1) * S, :]
            vbuf[k * S:(k + 1) * S, :] = pltpu.roll(jnp.where(sub == S - 1, prev, cur), 1, 0)
    else:
        vbuf[0:HALO * S, :] = cache_ref[...]
    for rows in _pieces((m + HALO) * S, 64):
        for lb in range(D_CONV // LANE):
            vblk[lb, rows, :] = vbuf[rows, lb * LANE:(lb + 1) * LANE]

    jb = CONV_JB
    per_lb = n_sub * (m // jb)
    trips_per_lb = per_lb // CONV_GROUPS_PER_TRIP
    n_trips = trips_per_lb * (D_CONV // LANE)
    tiles_per_trip = N_TILES // n_trips
    assert trips_per_lb * CONV_GROUPS_PER_TRIP == per_lb and tiles_per_trip * n_trips == N_TILES

    def conv_trip(i, c, head_partner):
        lb = i // trips_per_lb
        wv = [jnp.broadcast_to(convw_ref[lb, k:k + 1, :], (SUBLANE, LANE)) for k in range(CONV_WIDTH)]
        bias = jnp.broadcast_to(convb_ref[lb], (SUBLANE, LANE))
        for gq in range(CONV_GROUPS_PER_TRIP):
            r = (i % trips_per_lb) * CONV_GROUPS_PER_TRIP + gq
            base = pl.multiple_of((r % (m // jb)) * (jb * S) + (r // (m // jb)) * SUBLANE, SUBLANE)
            accs = [bias] * jb
            for t in range(jb + HALO):
                row = vblk[lb, pl.ds(base + t * S, SUBLANE), :]
                for q in range(jb):
                    k = t - q
                    if 0 <= k < CONV_WIDTH:
                        accs[q] = accs[q] + wv[k] * row
            for q in range(jb):
                cbuf[lb, pl.ds(base + q * S, SUBLANE), :] = accs[q]
        for tq in range(tiles_per_trip):
            tile = i * tiles_per_trip + tq
            res = _bdot(ub[tile // 4], wb_ref[tile])
            bu[tile] = res[:, 0:LANE]
            bu[N_TILES + tile] = res[:, LANE:2 * LANE]
        head_partner(i)
        return c

    n_pairs = n_pc // 2
    n_perm_trips = n_trips // 4
    chunks_per_trip = n_xc // n_perm_trips
    trips_per_pair = (n_trips - n_perm_trips) // n_pairs
    assert chunks_per_trip * n_perm_trips == n_xc and n_perm_trips + trips_per_pair * n_pairs == n_trips

    def interleave_partner(t):
        for q in range(chunks_per_trip):
            interleave_chunk(t * chunks_per_trip + q)

    lax.fori_loop(0, n_perm_trips, functools.partial(conv_trip, head_partner=interleave_partner), 0,
                  unroll=2)

    def pair_body(p, c):
        for q in range(trips_per_pair):
            conv_trip(n_perm_trips + p * trips_per_pair + q, c, head_partner=lambda t: None)
        project_chunk(2 * p, 1 - slot)
        project_chunk(2 * p + 1, 1 - slot)
        return c
    lax.fori_loop(0, n_pairs, pair_body, 0)

    def bu_get(rows, t0, nt, part):
        return jnp.concatenate([bu[part * N_TILES + t, rows, :] for t in range(t0, t0 + nt)], axis=1)

    def bu_set(rows, t0, part, val):
        for idx in range(val.shape[1] // LANE):
            bu[part * N_TILES + t0 + idx, rows, :] = val[:, idx * LANE:(idx + 1) * LANE]

    lch = (8 * 1024) // S
    nt = lch // LANE
    for lc in range(N_STATE // lch):
        l0 = lc * lch
        ar = jnp.broadcast_to(pwr_ref[0:1, l0:l0 + lch], (S, lch))
        ai = jnp.broadcast_to(pwi_ref[0:1, l0:l0 + lch], (S, lch))

        def scan_body(j, carry, t0=lc * nt, ar=ar, ai=ai):
            hr, hi = carry
            rows = pl.ds(pl.multiple_of(j * S, S), S)
            nr = ar * hr - ai * hi + bu_get(rows, t0, nt, 0)
            ni = ar * hi + ai * hr + bu_get(rows, t0, nt, 1)
            bu_set(rows, t0, 0, nr)
            bu_set(rows, t0, 1, ni)
            return nr, ni
        z = jnp.zeros((S, lch), F32)
        lax.fori_loop(0, m, scan_body, (z, z), unroll=2)

    if chain:
        amr = pwr_ref[m - 1:m, :]
        ami = pwi_ref[m - 1:m, :]
        cr = ire[0:1, :]
        ci = iim[0:1, :]
        for s in range(S):
            row = (m - 1) * S + s
            zr = bu_get(slice(row, row + 1), 0, N_TILES, 0)
            zi = bu_get(slice(row, row + 1), 0, N_TILES, 1)
            cr, ci = amr * cr - ami * ci + zr, amr * ci + ami * cr + zi
            if s < S - 1:
                ire[s + 1:s + 2, :] = cr
                iim[s + 1:s + 2, :] = ci
        car_r[...] = cr
        car_i[...] = ci

        @pl.when(i == last)
        def _():
            hr_out[...] = cr
            hi_out[...] = ci
    in_r = ire if chain else h0r_ref
    in_i = iim if chain else h0i_ref

    reps = BF16_ROWS // S if S < BF16_ROWS else 1
    tile_rows = reps * S
    for lc in range(N_STATE // 512):
        l0 = lc * 512
        e_r = jnp.concatenate([in_r[:, l0:l0 + 512]] * reps, axis=0).astype(BF16)
        e_i = jnp.concatenate([in_i[:, l0:l0 + 512]] * reps, axis=0).astype(BF16)

        def fix_body(jp, c, l0=l0, e_r=e_r, e_i=e_i):
            rows = pl.ds(pl.multiple_of(jp * tile_rows, tile_rows), tile_rows)
            pr = pwbr_ref[rows, l0:l0 + 512]
            pi = pwbi_ref[rows, l0:l0 + 512]
            hz_r = bu_get(rows, l0 // LANE, 512 // LANE, 0).astype(BF16)
            hz_i = bu_get(rows, l0 // LANE, 512 // LANE, 1).astype(BF16)
            hb[rows, l0:l0 + 512] = hz_r + (pr * e_r - pi * e_i)
            hb[rows, N_STATE + l0:N_STATE + l0 + 512] = hz_i + (pr * e_i + pi * e_r)
            return c
        lax.fori_loop(0, TB // tile_rows, fix_body, 0, unroll=2)

    if not chain:
        rows = slice((m - 1) * S, m * S)
        pr = pwr_ref[m - 1:m, :]
        pi = pwi_ref[m - 1:m, :]
        e_r = in_r[...]
        e_i = in_i[...]
        hr_out[...] = bu_get(rows, 0, N_TILES, 0) + pr * e_r - pi * e_i
        hi_out[...] = bu_get(rows, 0, N_TILES, 1) + pr * e_i + pi * e_r

    y0 = D_SSM + D_CONV
    for ob in range(D_SSM // LANE):
        acc = None
        for c in range(4 * ob, 4 * ob + 4):
            lhs = jnp.concatenate([hb[:, c * LANE:(c + 1) * LANE],
                                   hb[:, N_STATE + c * LANE:N_STATE + (c + 1) * LANE]], axis=1)
            t = _bdot(lhs, cw_ref[c])
            acc = t if acc is None else acc + t
        pset(slice(None), y0 + ob * LANE, acc)

    for rows in _pieces(TB, 64):
        z = _gelu_tanh(pget(rows, y0, D_SSM) + d_ref[...] * pget(rows, 0, D_SSM))
        pset(rows, y0, z)
        mixb[rows, 0:D_SSM] = z.astype(BF16)
    for n in range(D_SSM // PROJ_CHUNK):
        cols = slice(n * PROJ_CHUNK, (n + 1) * PROJ_CHUNK)
        pset(slice(None), D_SSM + n * PROJ_CHUNK, _bdot(mixb[:, 0:D_SSM], gluw_ref[:, cols]))
    for rows in _pieces(TB, 64):
        gate = jax.nn.sigmoid(pget(rows, D_SSM, D_CONV) + glub_ref[...])
        mixb[rows, 0:D_SSM] = (pget(rows, y0, D_SSM) * gate).astype(BF16)

    for rows in _pieces(TB, 64):
        conv = jnp.concatenate([cbuf[lb, rows, :] for lb in range(D_CONV // LANE)], axis=1)
        h = _layer_norm(conv, clng_ref[...], clnb_ref[...])
        mixb[rows, D_SSM:D_MODEL] = (h * jax.nn.sigmoid(h)).astype(BF16)

    if chain:
        @pl.when(i == last)
        def _():
            for k in range(HALO):
                row = (m + k) * S + S - 1
                conv_out[k:k + 1, :] = vbuf[row:row + 1, :]
    else:
        conv_out[...] = vbuf[m * S:(m + HALO) * S, :]

    xb[...] = _bdot(permt_ref[...], mixb[...]).astype(BF16)
    for n in range(n_xc):
        cols = slice(n * PROJ_CHUNK, (n + 1) * PROJ_CHUNK)
        mix_ref[:, cols] = _bdot(xb[...], wout_ref[:, cols])


def _const_spec(shape):
    nd = len(shape)
    return pl.BlockSpec(shape, lambda g: (0,) * nd, pipeline_mode=pl.Buffered(1))


def _interleave_perm(S, m):
    r = jnp.arange(S * m)
    p = jax.nn.one_hot((r % S) * m + r // S, S * m, dtype=BF16)
    return p, p.T


def _mixer(x, h0r, h0i, cache, wts, *, S, m, chain):
    nb, T, _ = x.shape
    TB = S * m
    nblk = T // TB
    assert T == nblk * TB and S % SUBLANE == 0 and m % CONV_JB == 0
    assert (not chain) or (S == SUBLANE and m >= HALO)
    s0 = h0r.shape[1]
    crow = cache.shape[1]
    n_blocks = nb * nblk
    nxt = lambda g: jnp.minimum(g + 1, n_blocks - 1)
    bspec = lambda shape: pl.BlockSpec((None,) + shape, lambda g: (g // nblk, 0, 0))
    xspec = pl.BlockSpec((None, TB, D_MODEL), lambda g: (g // nblk, g % nblk, 0))
    xnspec = pl.BlockSpec((None, TB, D_MODEL), lambda g: (nxt(g) // nblk, nxt(g) % nblk, 0))
    wts = _interleave_perm(S, m) + tuple(wts)
    in_specs = [xspec, xnspec, bspec((s0, N_STATE)), bspec((s0, N_STATE)), bspec((crow, D_CONV))]
    in_specs += [_const_spec(w.shape) for w in wts]
    out_specs = [xspec, bspec((s0, N_STATE)), bspec((s0, N_STATE)), bspec((crow, D_CONV))]
    out_shape = [jax.ShapeDtypeStruct((nb, T, D_MODEL), F32),
                 jax.ShapeDtypeStruct((nb, s0, N_STATE), F32),
                 jax.ShapeDtypeStruct((nb, s0, N_STATE), F32),
                 jax.ShapeDtypeStruct((nb, crow, D_CONV), F32)]
    scratch = [pltpu.VMEM((TB, D_MODEL), BF16),
               pltpu.VMEM((D_MODEL // PROJ_CHUNK, TB, PROJ_CHUNK), BF16),
               pltpu.VMEM((D_MODEL // PROJ_CHUNK, TB, PROJ_CHUNK), BF16),
               pltpu.VMEM((2, D_IN // PROJ_CHUNK, TB, PROJ_CHUNK), F32),
               pltpu.VMEM((D_SSM // LANE, TB, LANE), BF16),
               pltpu.VMEM((2 * N_TILES, TB, LANE), F32),
               pltpu.VMEM((TB, 2 * N_STATE), BF16),
               pltpu.VMEM(((m + HALO) * S, D_CONV), F32),
               pltpu.VMEM((HALO * S, D_CONV), F32),
               pltpu.VMEM((TB, D_MODEL), BF16),
               pltpu.VMEM((S, N_STATE), F32),
               pltpu.VMEM((S, N_STATE), F32),
               pltpu.VMEM((1, N_STATE), F32),
               pltpu.VMEM((1, N_STATE), F32),
               pltpu.VMEM((D_CONV // LANE, (m + HALO) * S, LANE), F32),
               pltpu.VMEM((D_CONV // LANE, TB, LANE), F32)]
    return pl.pallas_call(
        functools.partial(_mixer_kernel, S=S, m=m, chain=chain, blocks_per_stream=nblk),
        grid=(n_blocks,),
        in_specs=in_specs,
        out_specs=out_specs,
        out_shape=out_shape,
        scratch_shapes=scratch,
        compiler_params=pltpu.CompilerParams(dimension_semantics=("arbitrary",),
                                             vmem_limit_bytes=VMEM_LIMIT),
        name="mixer_chain" if chain else "mixer_streams",
    )(x, x, h0r, h0i, cache, *wts)


def _attend(q, kb, vb):
    outs = []
    for h in range(MEM_HEADS):
        cols = slice(h * HEAD_DIM, (h + 1) * HEAD_DIM)
        s = lax.dot_general(q[:, cols], kb[:, cols], (((1,), (1,)), ((), ())),
                            preferred_element_type=F32) * (HEAD_DIM ** -0.5)
        e = jnp.exp(s - jnp.max(s, axis=-1, keepdims=True))
        p = (e / jnp.sum(e, axis=-1, keepdims=True)).astype(BF16)
        outs.append(_bdot(p, vb[:, cols]).astype(BF16))
    return jnp.concatenate(outs, axis=1)


def _post_attention(halves, x1_ref, ob, wo_ref, ln2g, ln2b, w1_ref, b1_ref, w2_ref, b2_ref, ln3g, ln3b,
                    out_ref, fbuf, x2buf, xb, hidb):
    for hs in halves:
        fbuf[hs, :] = _bdot(ob[hs, :], wo_ref[...])
    for hs in halves:
        for rows in _pieces(hs.stop - hs.start, LN_ROWS):
            rows = slice(hs.start + rows.start, hs.start + rows.stop)
            x2 = _layer_norm(ALPHA * x1_ref[rows, :] + fbuf[rows, :], ln2g[...], ln2b[...])
            x2buf[rows, :] = x2
            xb[rows, :] = x2.astype(BF16)
        for n in range(D_FF // D_MODEL):
            cols = slice(n * D_MODEL, (n + 1) * D_MODEL)
            a = jnp.maximum(_bdot(xb[hs, :], w1_ref[:, cols]) + b1_ref[:, cols], 0.0)
            hidb[hs, cols] = (a * a).astype(BF16)
    for hs in halves:
        fbuf[hs, :] = _bdot(hidb[hs, :], w2_ref[...])
        for rows in _pieces(hs.stop - hs.start, LN_ROWS):
            rows = slice(hs.start + rows.start, hs.start + rows.stop)
            out_ref[rows, :] = _layer_norm(ALPHA * x2buf[rows, :] + fbuf[rows, :] + b2_ref[...],
                                           ln3g[...], ln3b[...])


def _mixer_residual_norm(rows_list, x_ref, mix_ref, ln1g, ln1b, x1buf, xb):
    for rows in rows_list:
        x1 = _layer_norm(ALPHA * x_ref[rows, :] + mix_ref[rows, :], ln1g[...], ln1b[...])
        x1buf[rows, :] = x1
        xb[rows, :] = x1.astype(BF16)


def _attn_mlp_prompt_kernel(x_ref, mix_ref, k_ref, v_ref, ln1g, ln1b, wq_ref, wo_ref, ln2g, ln2b,
                            w1_ref, b1_ref, w2_ref, b2_ref, ln3g, ln3b, out_ref,
                            xb, qb, ob, kb, vb, fbuf, x2buf, hidb, x1buf, *, blocks_per_stream):
    n_rows = x_ref.shape[0]
    halves = _pieces(n_rows, n_rows // 2)

    @pl.when(pl.program_id(0) % blocks_per_stream == 0)
    def _():
        kb[...] = k_ref[...].astype(BF16)
        vb[...] = v_ref[...].astype(BF16)

    for hs in halves:
        pieces = [slice(hs.start + r.start, hs.start + r.stop) for r in _pieces(hs.stop - hs.start, LN_ROWS)]
        _mixer_residual_norm(pieces, x_ref, mix_ref, ln1g, ln1b, x1buf, xb)
        qb[hs, :] = _bdot(xb[hs, :], wq_ref[...]).astype(BF16)
    for hs in halves:
        ob[hs, :] = _attend(qb[hs, :], kb, vb)
    _post_attention(halves, x1buf, ob, wo_ref, ln2g, ln2b, w1_ref, b1_ref, w2_ref, b2_ref, ln3g, ln3b,
                    out_ref, fbuf, x2buf, xb, hidb)


def _attn_mlp_streams_kernel(x_ref, mix_ref, k_ref, v_ref, ln1g, ln1b, wq_ref, wo_ref, ln2g, ln2b,
                             w1_ref, b1_ref, w2_ref, b2_ref, ln3g, ln3b, out_ref,
                             xb, qb, ob, kb, vb, fbuf, x2buf, hidb, x1buf, *, t_len):
    n_rows = x_ref.shape[0]
    s_id = pl.program_id(0)

    @pl.when(s_id == 0)
    def _():
        _mixer_residual_norm(_pieces(n_rows, LN_ROWS), x_ref, mix_ref, ln1g, ln1b, x1buf, xb)
        qb[...] = _bdot(xb[...], wq_ref[...]).astype(BF16)

    kb[...] = k_ref[...].astype(BF16)
    vb[...] = v_ref[...].astype(BF16)
    rows = pl.ds(pl.multiple_of(s_id * t_len, t_len), t_len)
    ob[rows, :] = _attend(qb[rows, :], kb, vb)

    @pl.when(s_id == pl.num_programs(0) - 1)
    def _():
        _post_attention(_pieces(n_rows, n_rows), x1buf, ob, wo_ref, ln2g, ln2b, w1_ref, b1_ref, w2_ref,
                        b2_ref, ln3g, ln3b, out_ref, fbuf, x2buf, xb, hidb)


def _attn_mlp_scratch(rows):
    return [pltpu.VMEM((rows, D_MODEL), BF16),
            pltpu.VMEM((rows, D_MODEL), BF16),
            pltpu.VMEM((rows, D_MODEL), BF16),
            pltpu.VMEM((N_MEM, D_MODEL), BF16),
            pltpu.VMEM((N_MEM, D_MODEL), BF16),
            pltpu.VMEM((rows, D_MODEL), F32),
            pltpu.VMEM((rows, D_MODEL), F32),
            pltpu.VMEM((rows, D_FF), BF16),
            pltpu.VMEM((rows, D_MODEL), F32)]


def _w1(shape):
    nd = len(shape)
    return pl.BlockSpec(shape, lambda i: (0,) * nd, pipeline_mode=pl.Buffered(1))


def _attn_mlp_prompt(x, mix, mem_k, mem_v, wts, rows_per_stream):
    n = x.shape[0]
    blk_per_stream = rows_per_stream // ROWS_B
    assert rows_per_stream % ROWS_B == 0
    kvspec = pl.BlockSpec((None, N_MEM, D_MODEL), lambda i: (i // blk_per_stream, 0, 0))
    xspec = pl.BlockSpec((ROWS_B, D_MODEL), lambda i: (i, 0))
    return pl.pallas_call(
        functools.partial(_attn_mlp_prompt_kernel, blocks_per_stream=blk_per_stream),
        grid=(n // ROWS_B,),
        in_specs=[xspec, xspec, kvspec, kvspec] + [_w1(w.shape) for w in wts],
        out_specs=xspec,
        out_shape=jax.ShapeDtypeStruct((n, D_MODEL), F32),
        scratch_shapes=_attn_mlp_scratch(ROWS_B),
        compiler_params=pltpu.CompilerParams(dimension_semantics=("arbitrary",),
                                             vmem_limit_bytes=VMEM_LIMIT),
        name="attn_mlp_prompt",
    )(x, mix, mem_k, mem_v, *wts)


def _attn_mlp_streams(x, mix, mem_k, mem_v, wts, t_len):
    n = x.shape[0]
    n_streams = n // t_len
    kvspec = pl.BlockSpec((None, N_MEM, D_MODEL), lambda s: (s, 0, 0))
    xspec = pl.BlockSpec((n, D_MODEL), lambda s: (0, 0))
    return pl.pallas_call(
        functools.partial(_attn_mlp_streams_kernel, t_len=t_len),
        grid=(n_streams,),
        in_specs=[xspec, xspec, kvspec, kvspec] + [_w1(w.shape) for w in wts],
        out_specs=xspec,
        out_shape=jax.ShapeDtypeStruct((n, D_MODEL), F32),
        scratch_shapes=_attn_mlp_scratch(n),
        compiler_params=pltpu.CompilerParams(dimension_semantics=("arbitrary",),
                                             vmem_limit_bytes=VMEM_LIMIT),
        name="attn_mlp_streams",
    )(x, mix, mem_k, mem_v, *wts)


def _s5_tables(a_re, a_im, log_dt, b_re, b_im, c_re, c_im, n_steps):
    dt = jnp.exp(log_dt)[:, None]
    mag = jnp.exp(a_re * dt)
    ang = a_im * dt
    ab_re = mag * jnp.cos(ang)
    ab_im = mag * jnp.sin(ang)
    den = a_re * a_re + a_im * a_im
    p = ab_re - 1.0
    q = ab_im
    k_re = ((p * a_re + q * a_im) / den)[..., None]
    k_im = ((q * a_re - p * a_im) / den)[..., None]
    bb_re = k_re * b_re - k_im * b_im
    bb_im = k_re * b_im + k_im * b_re

    steps = jnp.arange(1, n_steps + 1, dtype=F32)[:, None, None]
    pmag = jnp.exp(a_re * dt * steps)
    pang = a_im * dt * steps
    pw_re = (pmag * jnp.cos(pang)).reshape(n_steps, N_STATE)
    pw_im = (pmag * jnp.sin(pang)).reshape(n_steps, N_STATE)
    pw_re = pw_re.at[0].set(ab_re.reshape(N_STATE))
    pw_im = pw_im.at[0].set(ab_im.reshape(N_STATE))

    slot = jax.nn.one_hot(jnp.arange(N_GROUPS).reshape(N_TILES, 2) % 8, 8, dtype=F32)
    place_b = lambda b: jnp.einsum("cqnp,cqr->crpqn", b.reshape(N_TILES, 2, SSM_STATE, SSM_GROUP),
                                   slot).reshape(N_TILES, LANE, LANE)
    wb = jnp.concatenate([place_b(bb_re), place_b(bb_im)], axis=2).astype(BF16)
    place_c = lambda cm: jnp.einsum("cqpn,cqr->cqnrp", cm.reshape(N_TILES, 2, SSM_GROUP, SSM_STATE),
                                    slot).reshape(N_TILES, LANE, LANE)
    cw = jnp.concatenate([place_c(c_re), -place_c(c_im)], axis=1).astype(BF16)
    return wb, cw, pw_re, pw_im


def kernel(x_prompt, x_sample, state_ssm_re, state_ssm_im, cache_conv, cache_mem_k, cache_mem_v, mem_prompt, w_in, ssm_a_re, ssm_a_im, ssm_log_dt, ssm_b_re, ssm_b_im, ssm_c_re, ssm_c_im, ssm_d, glu_w, glu_b, conv_w, conv_b, conv_ln_g, conv_ln_b, w_out, ln1_g, ln1_b, mem_w_q, mem_w_k, mem_w_v, mem_w_o, ln2_g, ln2_b, mlp_w1, mlp_b1, mlp_w2, mlp_b2, ln3_g, ln3_b):
    assert w_in.shape[0] == DEPTH == 1
    nb, t_prompt, _ = x_prompt.shape
    n_dec, t_dec, _ = x_sample.shape
    row = lambda v: v[0].reshape(1, -1)

    wb, cw, pw_re, pw_im = _s5_tables(ssm_a_re[0], ssm_a_im[0], ssm_log_dt[0], ssm_b_re[0], ssm_b_im[0],
                                      ssm_c_re[0], ssm_c_im[0], max(PROMPT_M, t_dec))
    w_in_tiles = w_in[0].astype(BF16).reshape(D_MODEL // PROJ_CHUNK, PROJ_CHUNK, D_IN // PROJ_CHUNK,
                                              PROJ_CHUNK).transpose(2, 0, 1, 3)
    shared = (w_in_tiles, wb, cw)
    lane_blocks = lambda a: a.reshape(a.shape[0], -1, LANE).transpose(1, 0, 2)
    rest = (row(ssm_d), glu_w[0].astype(BF16), row(glu_b), lane_blocks(conv_w[0]),
            lane_blocks(row(conv_b)), row(conv_ln_g),
            row(conv_ln_b), w_out[0].astype(BF16))
    per_row = lambda t, n_steps, S: jnp.repeat(t[:n_steps], S, axis=0).astype(BF16)
    mixer_weights = lambda n_steps, S: shared + (
        pw_re[:n_steps], pw_im[:n_steps], per_row(pw_re, n_steps, S), per_row(pw_im, n_steps, S)) + rest

    post_weights = (row(ln1_g), row(ln1_b),
                    mem_w_q[0].astype(BF16), mem_w_o[0].astype(BF16), row(ln2_g), row(ln2_b),
                    mlp_w1[0].astype(BF16), row(mlp_b1), mlp_w2[0].astype(BF16), row(mlp_b2),
                    row(ln3_g), row(ln3_b))

    mk, mv = _memory_kv(mem_prompt, mem_w_k[0].astype(BF16), mem_w_v[0].astype(BF16))
    zero_h = jnp.zeros((nb, 1, N_STATE), F32)
    zero_conv = jnp.zeros((nb, HALO, D_CONV), x_prompt.dtype)
    x1p, p_re, p_im, p_conv = _mixer(x_prompt, zero_h, zero_h, zero_conv, mixer_weights(PROMPT_M, PROMPT_S),
                                     S=PROMPT_S, m=PROMPT_M, chain=True)
    yp = _attn_mlp_prompt(x_prompt.reshape(nb * t_prompt, D_MODEL), x1p.reshape(nb * t_prompt, D_MODEL),
                          mk, mv, post_weights, t_prompt)
    yp = yp.reshape(nb, t_prompt, D_MODEL)

    cache_sj = cache_conv[0].transpose(1, 0, 2).reshape(1, HALO * n_dec, D_CONV)
    x1s, s_re, s_im, s_conv = _mixer(
        x_sample.reshape(1, n_dec * t_dec, D_MODEL),
        state_ssm_re[0].reshape(1, n_dec, N_STATE), state_ssm_im[0].reshape(1, n_dec, N_STATE),
        cache_sj, mixer_weights(t_dec, n_dec), S=n_dec, m=t_dec, chain=False)
    ys = _attn_mlp_streams(x_sample.reshape(n_dec * t_dec, D_MODEL), x1s.reshape(n_dec * t_dec, D_MODEL),
                           cache_mem_k[0].reshape(n_dec, N_MEM, D_MODEL),
                           cache_mem_v[0].reshape(n_dec, N_MEM, D_MODEL), post_weights, t_dec)
    ys = ys.reshape(n_dec, t_dec, D_MODEL)
    s_conv = s_conv.reshape(HALO, n_dec, D_CONV).transpose(1, 0, 2)

    st = lambda a, n: a.reshape(1, n, N_GROUPS, SSM_STATE)
    kv5 = lambda a: a.reshape(1, nb, N_MEM, MEM_HEADS, HEAD_DIM)
    return (yp, ys, st(p_re, nb), st(p_im, nb), p_conv.reshape(1, nb, HALO, D_CONV), kv5(mk), kv5(mv),
            st(s_re, n_dec), st(s_im, n_dec), s_conv.reshape(1, n_dec, HALO, D_CONV))
```

```python
import functools
import math

import jax
import jax.numpy as jnp
from jax import lax
from jax.experimental import pallas as pl
from jax.experimental.pallas import tpu as pltpu

F32 = jnp.float32
BF16 = jnp.bfloat16

D_MODEL = 1024
D_SSM = 512
D_CONV = 512
SSM_GROUP = 16
N_GROUPS = 32
SSM_STATE = 64
N_STATE = N_GROUPS * SSM_STATE
CONV_WIDTH = 31
HALO = CONV_WIDTH - 1
N_MEM = 256
MEM_HEADS = 4
HEAD_DIM = 256
D_FF = 4096
D_IN = D_SSM + 2 * D_CONV
LN_EPS = 1e-5
DEPTH = 1
ALPHA = (2.0 * DEPTH) ** 0.25

LANE = 128
SUBLANE = 8
BF16_ROWS = 16
N_TILES = N_STATE // LANE
VMEM_LIMIT = 56 * 1024 * 1024

PROMPT_S = 8
PROMPT_M = 64
ROWS_B = 512
LN_ROWS = 32
CONV_JB = 8
CONV_GROUPS_PER_TRIP = 2
PROJ_CHUNK = 256


def _layer_norm(x, g, b):
    mu = jnp.mean(x, axis=-1, keepdims=True)
    xc = x - mu
    var = jnp.mean(xc * xc, axis=-1, keepdims=True)
    return xc * lax.rsqrt(var + LN_EPS) * g + b


def _gelu_tanh(x):
    return 0.5 * x * (1.0 + jnp.tanh(math.sqrt(2.0 / math.pi) * (x + 0.044715 * (x * x * x))))


def _pieces(n_rows, chunk):
    chunk = min(chunk, n_rows)
    return [slice(r0, r0 + chunk) for r0 in range(0, n_rows, chunk)]


def _bdot(a, b):
    return jnp.dot(a, b, preferred_element_type=F32)


def _kv_kernel(mem_ref, wk_ref, wv_ref, k_ref, v_ref):
    mb = mem_ref[...].astype(BF16)
    k_ref[...] = _bdot(mb, wk_ref[...])
    v_ref[...] = _bdot(mb, wv_ref[...])


def _memory_kv(mem, wk, wv):
    nb = mem.shape[0]
    wspec = pl.BlockSpec((D_MODEL, D_MODEL), lambda b: (0, 0), pipeline_mode=pl.Buffered(1))
    ospec = pl.BlockSpec((None, N_MEM, D_MODEL), lambda b: (b, 0, 0))
    return pl.pallas_call(
        _kv_kernel,
        grid=(nb,),
        in_specs=[pl.BlockSpec((None, N_MEM, D_MODEL), lambda b: (b, 0, 0)), wspec, wspec],
        out_specs=[ospec, ospec],
        out_shape=[jax.ShapeDtypeStruct((nb, N_MEM, D_MODEL), F32)] * 2,
        compiler_params=pltpu.CompilerParams(dimension_semantics=("arbitrary",),
                                             vmem_limit_bytes=VMEM_LIMIT),
        name="memory_kv",
    )(mem, wk, wv)


def _mixer_kernel(x_ref, xn_ref, h0r_ref, h0i_ref, cache_ref, perm_ref, permt_ref, w_in_ref, wb_ref, cw_ref,
                  pwr_ref, pwi_ref, pwbr_ref, pwbi_ref, d_ref, gluw_ref, glub_ref, convw_ref, convb_ref, clng_ref, clnb_ref,
                  wout_ref,
                  mix_ref, hr_out, hi_out, conv_out,
                  xb, xbn, xpn, proj, ub, bu, hb, vbuf, tail, mixb, ire, iim, car_r, car_i, vblk, cbuf,
                  *, S, m, chain, blocks_per_stream):
    TB = S * m
    g = pl.program_id(0)
    i = g % blocks_per_stream
    last = blocks_per_stream - 1
    slot = g % 2
    n_sub = S // SUBLANE
    n_xc = D_MODEL // PROJ_CHUNK
    n_pc = D_IN // PROJ_CHUNK

    def pget(rows, col0, ncols):
        out, c = [], col0
        while c < col0 + ncols:
            ch, off = divmod(c, PROJ_CHUNK)
            w = min(PROJ_CHUNK - off, col0 + ncols - c)
            out.append(proj[slot, ch, rows, off:off + w])
            c += w
        return out[0] if len(out) == 1 else jnp.concatenate(out, axis=1)

    def pset(rows, col0, val):
        c = 0
        while c < val.shape[1]:
            ch, off = divmod(col0 + c, PROJ_CHUNK)
            w = min(PROJ_CHUNK - off, val.shape[1] - c)
            proj[slot, ch, rows, off:off + w] = val[:, c:c + w]
            c += w

    def cast_chunks(src):
        for rows in _pieces(TB, 64):
            for c in range(n_xc):
                xbn[c, rows, :] = src[rows, c * PROJ_CHUNK:(c + 1) * PROJ_CHUNK].astype(BF16)

    def interleave_chunk(c):
        xpn[c] = _bdot(perm_ref[...], xbn[c]).astype(BF16)

    def project_chunk(n, dst_slot):
        acc = None
        for k in range(n_xc):
            t = _bdot(xpn[k], w_in_ref[n, k])
            acc = t if acc is None else acc + t
        proj[dst_slot, n] = acc

    @pl.when(g == 0)
    def _():
        cast_chunks(x_ref)
        for c in range(n_xc):
            interleave_chunk(c)
        for n in range(n_pc):
            project_chunk(n, 0)

    cast_chunks(xn_ref)

    if chain:
        @pl.when(i == 0)
        def _():
            for k in range(HALO):
                tail[k * S:(k + 1) * S, :] = jnp.broadcast_to(cache_ref[k:k + 1, :], (S, D_CONV))
            ire[0:1, :] = h0r_ref[...]
            iim[0:1, :] = h0i_ref[...]

        @pl.when(i > 0)
        def _():
            tail[...] = vbuf[m * S:(m + HALO) * S, :]
            ire[0:1, :] = car_r[...]
            iim[0:1, :] = car_i[...]

    for rows in _pieces(TB, 64):
        for kb in range(D_SSM // LANE):
            ub[kb, rows, :] = pget(rows, kb * LANE, LANE).astype(BF16)
        vbuf[HALO * S + rows.start:HALO * S + rows.stop, :] = (
            pget(rows, D_SSM, D_CONV) * jax.nn.sigmoid(pget(rows, D_SSM + D_CONV, D_CONV)))

    if chain:
        sub = lax.broadcasted_iota(jnp.int32, (S, D_CONV), 0)
        for k in range(HALO):
            cur = vbuf[(m + k) * S:(m + k + 1) * S, :]
            prev = tail[k * S:(k + 1) * S, :]
            vbuf[k * S:(k + 1) * S, :] = pltpu.roll(jnp.where(sub == S - 1, prev, cur), 1, 0)
    else:
        vbuf[0:HALO * S, :] = cache_ref[...]
    for rows in _pieces((m + HALO) * S, 64):
        for lb in range(D_CONV // LANE):
            vblk[lb, rows, :] = vbuf[rows, lb * LANE:(lb + 1) * LANE]

    jb = CONV_JB
    per_lb = n_sub * (m // jb)
    trips_per_lb = per_lb // CONV_GROUPS_PER_TRIP
    n_trips = trips_per_lb * (D_CONV // LANE)
    tiles_per_trip = N_TILES // n_trips
    assert trips_per_lb * CONV_GROUPS_PER_TRIP == per_lb and tiles_per_trip * n_trips == N_TILES

    def conv_trip(i, c, head_partner):
        lb = i // trips_per_lb
        wv = [jnp.broadcast_to(convw_ref[lb, k:k + 1, :], (SUBLANE, LANE)) for k in range(CONV_WIDTH)]
        bias = jnp.broadcast_to(convb_ref[lb], (SUBLANE, LANE))
        for gq in range(CONV_GROUPS_PER_TRIP):
            r = (i % trips_per_lb) * CONV_GROUPS_PER_TRIP + gq
            base = pl.multiple_of((r % (m // jb)) * (jb * S) + (r // (m // jb)) * SUBLANE, SUBLANE)
            accs = [bias] * jb
            for t in range(jb + HALO):
                row = vblk[lb, pl.ds(base + t * S, SUBLANE), :]
                for q in range(jb):
                    k = t - q
                    if 0 <= k < CONV_WIDTH:
                        accs[q] = accs[q] + wv[k] * row
            for q in range(jb):
                cbuf[lb, pl.ds(base + q * S, SUBLANE), :] = accs[q]
        for tq in range(tiles_per_trip):
            tile = i * tiles_per_trip + tq
            res = _bdot(ub[tile // 4], wb_ref[tile])
            bu[tile] = res[:, 0:LANE]
            bu[N_TILES + tile] = res[:, LANE:2 * LANE]
        head_partner(i)
        return c

    n_pairs = n_pc // 2
    n_perm_trips = n_trips // 4
    chunks_per_trip = n_xc // n_perm_trips
    trips_per_pair = (n_trips - n_perm_trips) // n_pairs
    assert chunks_per_trip * n_perm_trips == n_xc and n_perm_trips + trips_per_pair * n_pairs == n_trips

    def interleave_partner(t):
        for q in range(chunks_per_trip):
            interleave_chunk(t * chunks_per_trip + q)

    lax.fori_loop(0, n_perm_trips, functools.partial(conv_trip, head_partner=interleave_partner), 0,
                  unroll=2)

    def pair_body(p, c):
        for q in range(trips_per_pair):
            conv_trip(n_perm_trips + p * trips_per_pair + q, c, head_partner=lambda t: None)
        project_chunk(2 * p, 1 - slot)
        project_chunk(2 * p + 1, 1 - slot)
        return c
    lax.fori_loop(0, n_pairs, pair_body, 0, unroll=True)

    def bu_get(rows, t0, nt, part):
        return jnp.concatenate([bu[part * N_TILES + t, rows, :] for t in range(t0, t0 + nt)], axis=1)

    def bu_set(rows, t0, part, val):
        for idx in range(val.shape[1] // LANE):
            bu[part * N_TILES + t0 + idx, rows, :] = val[:, idx * LANE:(idx + 1) * LANE]

    lch = (8 * 1024) // S
    nt = lch // LANE
    for lc in range(N_STATE // lch):
        l0 = lc * lch
        ar = jnp.broadcast_to(pwr_ref[0:1, l0:l0 + lch], (S, lch))
        ai = jnp.broadcast_to(pwi_ref[0:1, l0:l0 + lch], (S, lch))

        def scan_body(j, carry, t0=lc * nt, ar=ar, ai=ai):
            hr, hi = carry
            rows = pl.ds(pl.multiple_of(j * S, S), S)
            nr = ar * hr - ai * hi + bu_get(rows, t0, nt, 0)
            ni = ar * hi + ai * hr + bu_get(rows, t0, nt, 1)
            bu_set(rows, t0, 0, nr)
            bu_set(rows, t0, 1, ni)
            return nr, ni
        z = jnp.zeros((S, lch), F32)
        lax.fori_loop(0, m, scan_body, (z, z), unroll=2)

    if chain:
        amr = pwr_ref[m - 1:m, :]
        ami = pwi_ref[m - 1:m, :]
        cr = ire[0:1, :]
        ci = iim[0:1, :]
        for s in range(S):
            row = (m - 1) * S + s
            zr = bu_get(slice(row, row + 1), 0, N_TILES, 0)
            zi = bu_get(slice(row, row + 1), 0, N_TILES, 1)
            cr, ci = amr * cr - ami * ci + zr, amr * ci + ami * cr + zi
            if s < S - 1:
                ire[s + 1:s + 2, :] = cr
                iim[s + 1:s + 2, :] = ci
        car_r[...] = cr
        car_i[...] = ci

        @pl.when(i == last)
        def _():
            hr_out[...] = cr
            hi_out[...] = ci
    in_r = ire if chain else h0r_ref
    in_i = iim if chain else h0i_ref

    reps = BF16_ROWS // S if S < BF16_ROWS else 1
    tile_rows = reps * S
    for lc in range(N_STATE // 512):
        l0 = lc * 512
        e_r = jnp.concatenate([in_r[:, l0:l0 + 512]] * reps, axis=0).astype(BF16)
        e_i = jnp.concatenate([in_i[:, l0:l0 + 512]] * reps, axis=0).astype(BF16)

        def fix_body(jp, c, l0=l0, e_r=e_r, e_i=e_i):
            rows = pl.ds(pl.multiple_of(jp * tile_rows, tile_rows), tile_rows)
            pr = pwbr_ref[rows, l0:l0 + 512]
            pi = pwbi_ref[rows, l0:l0 + 512]
            hz_r = bu_get(rows, l0 // LANE, 512 // LANE, 0).astype(BF16)
            hz_i = bu_get(rows, l0 // LANE, 512 // LANE, 1).astype(BF16)
            hb[rows, l0:l0 + 512] = hz_r + (pr * e_r - pi * e_i)
            hb[rows, N_STATE + l0:N_STATE + l0 + 512] = hz_i + (pr * e_i + pi * e_r)
            return c
        lax.fori_loop(0, TB // tile_rows, fix_body, 0, unroll=4)

    if not chain:
        rows = slice((m - 1) * S, m * S)
        pr = pwr_ref[m - 1:m, :]
        pi = pwi_ref[m - 1:m, :]
        e_r = in_r[...]
        e_i = in_i[...]
        hr_out[...] = bu_get(rows, 0, N_TILES, 0) + pr * e_r - pi * e_i
        hi_out[...] = bu_get(rows, 0, N_TILES, 1) + pr * e_i + pi * e_r

    y0 = D_SSM + D_CONV
    for ob in range(D_SSM // LANE):
        acc = None
        for c in range(4 * ob, 4 * ob + 4):
            lhs = jnp.concatenate([hb[:, c * LANE:(c + 1) * LANE],
                                   hb[:, N_STATE + c * LANE:N_STATE + (c + 1) * LANE]], axis=1)
            t = _bdot(lhs, cw_ref[c])
            acc = t if acc is None else acc + t
        pset(slice(None), y0 + ob * LANE, acc)

    for rows in _pieces(TB, 64):
        z = _gelu_tanh(pget(rows, y0, D_SSM) + d_ref[...] * pget(rows, 0, D_SSM))
        pset(rows, y0, z)
        mixb[rows, 0:D_SSM] = z.astype(BF16)
    for n in range(D_SSM // PROJ_CHUNK):
        cols = slice(n * PROJ_CHUNK, (n + 1) * PROJ_CHUNK)
        pset(slice(None), D_SSM + n * PROJ_CHUNK, _bdot(mixb[:, 0:D_SSM], gluw_ref[:, cols]))
    for rows in _pieces(TB, 64):
        gate = jax.nn.sigmoid(pget(rows, D_SSM, D_CONV) + glub_ref[...])
        mixb[rows, 0:D_SSM] = (pget(rows, y0, D_SSM) * gate).astype(BF16)

    for rows in _pieces(TB, 64):
        conv = jnp.concatenate([cbuf[lb, rows, :] for lb in range(D_CONV // LANE)], axis=1)
        h = _layer_norm(conv, clng_ref[...], clnb_ref[...])
        mixb[rows, D_SSM:D_MODEL] = (h * jax.nn.sigmoid(h)).astype(BF16)

    if chain:
        @pl.when(i == last)
        def _():
            for k in range(HALO):
                row = (m + k) * S + S - 1
                conv_out[k:k + 1, :] = vbuf[row:row + 1, :]
    else:
        conv_out[...] = vbuf[m * S:(m + HALO) * S, :]

    xb[...] = _bdot(permt_ref[...], mixb[...]).astype(BF16)
    for n in range(n_xc):
        cols = slice(n * PROJ_CHUNK, (n + 1) * PROJ_CHUNK)
        mix_ref[:, cols] = _bdot(xb[...], wout_ref[:, cols])


def _const_spec(shape):
    nd = len(shape)
    return pl.BlockSpec(shape, lambda g: (0,) * nd, pipeline_mode=pl.Buffered(1))


def _interleave_perm(S, m):
    r = jnp.arange(S * m)
    p = jax.nn.one_hot((r % S) * m + r // S, S * m, dtype=BF16)
    return p, p.T


def _mixer(x, h0r, h0i, cache, wts, *, S, m, chain):
    nb, T, _ = x.shape
    TB = S * m
    nblk = T // TB
    assert T == nblk * TB and S % SUBLANE == 0 and m % CONV_JB == 0
    assert (not chain) or (S == SUBLANE and m >= HALO)
    s0 = h0r.shape[1]
    crow = cache.shape[1]
    n_blocks = nb * nblk
    nxt = lambda g: jnp.minimum(g + 1, n_blocks - 1)
    bspec = lambda shape: pl.BlockSpec((None,) + shape, lambda g: (g // nblk, 0, 0))
    xspec = pl.BlockSpec((None, TB, D_MODEL), lambda g: (g // nblk, g % nblk, 0))
    xnspec = pl.BlockSpec((None, TB, D_MODEL), lambda g: (nxt(g) // nblk, nxt(g) % nblk, 0))
    wts = _interleave_perm(S, m) + tuple(wts)
    in_specs = [xspec, xnspec, bspec((s0, N_STATE)), bspec((s0, N_STATE)), bspec((crow, D_CONV))]
    in_specs += [_const_spec(w.shape) for w in wts]
    out_specs = [xspec, bspec((s0, N_STATE)), bspec((s0, N_STATE)), bspec((crow, D_CONV))]
    out_shape = [jax.ShapeDtypeStruct((nb, T, D_MODEL), F32),
                 jax.ShapeDtypeStruct((nb, s0, N_STATE), F32),
                 jax.ShapeDtypeStruct((nb, s0, N_STATE), F32),
                 jax.ShapeDtypeStruct((nb, crow, D_CONV), F32)]
    scratch = [pltpu.VMEM((TB, D_MODEL), BF16),
               pltpu.VMEM((D_MODEL // PROJ_CHUNK, TB, PROJ_CHUNK), BF16),
               pltpu.VMEM((D_MODEL // PROJ_CHUNK, TB, PROJ_CHUNK), BF16),
               pltpu.VMEM((2, D_IN // PROJ_CHUNK, TB, PROJ_CHUNK), F32),
               pltpu.VMEM((D_SSM // LANE, TB, LANE), BF16),
               pltpu.VMEM((2 * N_TILES, TB, LANE), F32),
               pltpu.VMEM((TB, 2 * N_STATE), BF16),
               pltpu.VMEM(((m + HALO) * S, D_CONV), F32),
               pltpu.VMEM((HALO * S, D_CONV), F32),
               pltpu.VMEM((TB, D_MODEL), BF16),
               pltpu.VMEM((S, N_STATE), F32),
               pltpu.VMEM((S, N_STATE), F32),
               pltpu.VMEM((1, N_STATE), F32),
               pltpu.VMEM((1, N_STATE), F32),
               pltpu.VMEM((D_CONV // LANE, (m + HALO) * S, LANE), F32),
               pltpu.VMEM((D_CONV // LANE, TB, LANE), F32)]
    return pl.pallas_call(
        functools.partial(_mixer_kernel, S=S, m=m, chain=chain, blocks_per_stream=nblk),
        grid=(n_blocks,),
        in_specs=in_specs,
        out_specs=out_specs,
        out_shape=out_shape,
        scratch_shapes=scratch,
        compiler_params=pltpu.CompilerParams(dimension_semantics=("arbitrary",),
                                             vmem_limit_bytes=VMEM_LIMIT),
        name="mixer_chain" if chain else "mixer_streams",
    )(x, x, h0r, h0i, cache, *wts)


def _attend(q, kb, vb):
    outs = []
    for h in range(MEM_HEADS):
        cols = slice(h * HEAD_DIM, (h + 1) * HEAD_DIM)
        s = lax.dot_general(q[:, cols], kb[:, cols], (((1,), (1,)), ((), ())),
                            preferred_element_type=F32) * (HEAD_DIM ** -0.5)
        e = jnp.exp(s - jnp.max(s, axis=-1, keepdims=True))
        p = (e / jnp.sum(e, axis=-1, keepdims=True)).astype(BF16)
        outs.append(_bdot(p, vb[:, cols]).astype(BF16))
    return jnp.concatenate(outs, axis=1)


def _post_attention(halves, x1_ref, ob, wo_ref, ln2g, ln2b, w1_ref, b1_ref, w2_ref, b2_ref, ln3g, ln3b,
                    out_ref, fbuf, x2buf, xb, hidb):
    for hs in halves:
        fbuf[hs, :] = _bdot(ob[hs, :], wo_ref[...])
    for hs in halves:
        for rows in _pieces(hs.stop - hs.start, LN_ROWS):
            rows = slice(hs.start + rows.start, hs.start + rows.stop)
            x2 = _layer_norm(ALPHA * x1_ref[rows, :] + fbuf[rows, :], ln2g[...], ln2b[...])
            x2buf[rows, :] = x2
            xb[rows, :] = x2.astype(BF16)
        for n in range(D_FF // D_MODEL):
            cols = slice(n * D_MODEL, (n + 1) * D_MODEL)
            a = jnp.maximum(_bdot(xb[hs, :], w1_ref[:, cols]) + b1_ref[:, cols], 0.0)
            hidb[hs, cols] = (a * a).astype(BF16)
    for hs in halves:
        fbuf[hs, :] = _bdot(hidb[hs, :], w2_ref[...])
        for rows in _pieces(hs.stop - hs.start, LN_ROWS):
            rows = slice(hs.start + rows.start, hs.start + rows.stop)
            out_ref[rows, :] = _layer_norm(ALPHA * x2buf[rows, :] + fbuf[rows, :] + b2_ref[...],
                                           ln3g[...], ln3b[...])


def _mixer_residual_norm(rows_list, x_ref, mix_ref, ln1g, ln1b, x1buf, xb):
    for rows in rows_list:
        x1 = _layer_norm(ALPHA * x_ref[rows, :] + mix_ref[rows, :], ln1g[...], ln1b[...])
        x1buf[rows, :] = x1
        xb[rows, :] = x1.astype(BF16)


def _attn_mlp_prompt_kernel(x_ref, mix_ref, k_ref, v_ref, ln1g, ln1b, wq_ref, wo_ref, ln2g, ln2b,
                            w1_ref, b1_ref, w2_ref, b2_ref, ln3g, ln3b, out_ref,
                            xb, qb, ob, kb, vb, fbuf, x2buf, hidb, x1buf, *, blocks_per_stream):
    n_rows = x_ref.shape[0]
    halves = _pieces(n_rows, n_rows // 2)

    @pl.when(pl.program_id(0) % blocks_per_stream == 0)
    def _():
        kb[...] = k_ref[...].astype(BF16)
        vb[...] = v_ref[...].astype(BF16)

    for hs in halves:
        pieces = [slice(hs.start + r.start, hs.start + r.stop) for r in _pieces(hs.stop - hs.start, LN_ROWS)]
        _mixer_residual_norm(pieces, x_ref, mix_ref, ln1g, ln1b, x1buf, xb)
        qb[hs, :] = _bdot(xb[hs, :], wq_ref[...]).astype(BF16)
    for hs in halves:
        ob[hs, :] = _attend(qb[hs, :], kb, vb)
    _post_attention(halves, x1buf, ob, wo_ref, ln2g, ln2b, w1_ref, b1_ref, w2_ref, b2_ref, ln3g, ln3b,
                    out_ref, fbuf, x2buf, xb, hidb)


def _attn_mlp_streams_kernel(x_ref, mix_ref, k_ref, v_ref, ln1g, ln1b, wq_ref, wo_ref, ln2g, ln2b,
                             w1_ref, b1_ref, w2_ref, b2_ref, ln3g, ln3b, out_ref,
                             xb, qb, ob, kb, vb, fbuf, x2buf, hidb, x1buf, *, t_len):
    n_rows = x_ref.shape[0]
    s_id = pl.program_id(0)

    @pl.when(s_id == 0)
    def _():
        _mixer_residual_norm(_pieces(n_rows, LN_ROWS), x_ref, mix_ref, ln1g, ln1b, x1buf, xb)
        qb[...] = _bdot(xb[...], wq_ref[...]).astype(BF16)

    kb[...] = k_ref[...].astype(BF16)
    vb[...] = v_ref[...].astype(BF16)
    rows = pl.ds(pl.multiple_of(s_id * t_len, t_len), t_len)
    ob[rows, :] = _attend(qb[rows, :], kb, vb)

    @pl.when(s_id == pl.num_programs(0) - 1)
    def _():
        _post_attention(_pieces(n_rows, n_rows), x1buf, ob, wo_ref, ln2g, ln2b, w1_ref, b1_ref, w2_ref,
                        b2_ref, ln3g, ln3b, out_ref, fbuf, x2buf, xb, hidb)


def _attn_mlp_scratch(rows):
    return [pltpu.VMEM((rows, D_MODEL), BF16),
            pltpu.VMEM((rows, D_MODEL), BF16),
            pltpu.VMEM((rows, D_MODEL), BF16),
            pltpu.VMEM((N_MEM, D_MODEL), BF16),
            pltpu.VMEM((N_MEM, D_MODEL), BF16),
            pltpu.VMEM((rows, D_MODEL), F32),
            pltpu.VMEM((rows, D_MODEL), F32),
            pltpu.VMEM((rows, D_FF), BF16),
            pltpu.VMEM((rows, D_MODEL), F32)]


def _w1(shape):
    nd = len(shape)
    return pl.BlockSpec(shape, lambda i: (0,) * nd, pipeline_mode=pl.Buffered(1))


def _attn_mlp_prompt(x, mix, mem_k, mem_v, wts, rows_per_stream):
    n = x.shape[0]
    blk_per_stream = rows_per_stream // ROWS_B
    assert rows_per_stream % ROWS_B == 0
    kvspec = pl.BlockSpec((None, N_MEM, D_MODEL), lambda i: (i // blk_per_stream, 0, 0))
    xspec = pl.BlockSpec((ROWS_B, D_MODEL), lambda i: (i, 0))
    return pl.pallas_call(
        functools.partial(_attn_mlp_prompt_kernel, blocks_per_stream=blk_per_stream),
        grid=(n // ROWS_B,),
        in_specs=[xspec, xspec, kvspec, kvspec] + [_w1(w.shape) for w in wts],
        out_specs=xspec,
        out_shape=jax.ShapeDtypeStruct((n, D_MODEL), F32),
        scratch_shapes=_attn_mlp_scratch(ROWS_B),
        compiler_params=pltpu.CompilerParams(dimension_semantics=("arbitrary",),
                                             vmem_limit_bytes=VMEM_LIMIT),
        name="attn_mlp_prompt",
    )(x, mix, mem_k, mem_v, *wts)


def _attn_mlp_streams(x, mix, mem_k, mem_v, wts, t_len):
    n = x.shape[0]
    n_streams = n // t_len
    kvspec = pl.BlockSpec((None, N_MEM, D_MODEL), lambda s: (s, 0, 0))
    xspec = pl.BlockSpec((n, D_MODEL), lambda s: (0, 0))
    return pl.pallas_call(
        functools.partial(_attn_mlp_streams_kernel, t_len=t_len),
        grid=(n_streams,),
        in_specs=[xspec, xspec, kvspec, kvspec] + [_w1(w.shape) for w in wts],
        out_specs=xspec,
        out_shape=jax.ShapeDtypeStruct((n, D_MODEL), F32),
        scratch_shapes=_attn_mlp_scratch(n),
        compiler_params=pltpu.CompilerParams(dimension_semantics=("arbitrary",),
                                             vmem_limit_bytes=VMEM_LIMIT),
        name="attn_mlp_streams",
    )(x, mix, mem_k, mem_v, *wts)


def _s5_tables(a_re, a_im, log_dt, b_re, b_im, c_re, c_im, n_steps):
    dt = jnp.exp(log_dt)[:, None]
    mag = jnp.exp(a_re * dt)
    ang = a_im * dt
    ab_re = mag * jnp.cos(ang)
    ab_im = mag * jnp.sin(ang)
    den = a_re * a_re + a_im * a_im
    p = ab_re - 1.0
    q = ab_im
    k_re = ((p * a_re + q * a_im) / den)[..., None]
    k_im = ((q * a_re - p * a_im) / den)[..., None]
    bb_re = k_re * b_re - k_im * b_im
    bb_im = k_re * b_im + k_im * b_re

    steps = jnp.arange(1, n_steps + 1, dtype=F32)[:, None, None]
    pmag = jnp.exp(a_re * dt * steps)
    pang = a_im * dt * steps
    pw_re = (pmag * jnp.cos(pang)).reshape(n_steps, N_STATE)
    pw_im = (pmag * jnp.sin(pang)).reshape(n_steps, N_STATE)
    pw_re = pw_re.at[0].set(ab_re.reshape(N_STATE))
    pw_im = pw_im.at[0].set(ab_im.reshape(N_STATE))

    slot = jax.nn.one_hot(jnp.arange(N_GROUPS).reshape(N_TILES, 2) % 8, 8, dtype=F32)
    place_b = lambda b: jnp.einsum("cqnp,cqr->crpqn", b.reshape(N_TILES, 2, SSM_STATE, SSM_GROUP),
                                   slot).reshape(N_TILES, LANE, LANE)
    wb = jnp.concatenate([place_b(bb_re), place_b(bb_im)], axis=2).astype(BF16)
    place_c = lambda cm: jnp.einsum("cqpn,cqr->cqnrp", cm.reshape(N_TILES, 2, SSM_GROUP, SSM_STATE),
                                    slot).reshape(N_TILES, LANE, LANE)
    cw = jnp.concatenate([place_c(c_re), -place_c(c_im)], axis=1).astype(BF16)
    return wb, cw, pw_re, pw_im


def kernel(x_prompt, x_sample, state_ssm_re, state_ssm_im, cache_conv, cache_mem_k, cache_mem_v, mem_prompt, w_in, ssm_a_re, ssm_a_im, ssm_log_dt, ssm_b_re, ssm_b_im, ssm_c_re, ssm_c_im, ssm_d, glu_w, glu_b, conv_w, conv_b, conv_ln_g, conv_ln_b, w_out, ln1_g, ln1_b, mem_w_q, mem_w_k, mem_w_v, mem_w_o, ln2_g, ln2_b, mlp_w1, mlp_b1, mlp_w2, mlp_b2, ln3_g, ln3_b):
    assert w_in.shape[0] == DEPTH == 1
    nb, t_prompt, _ = x_prompt.shape
    n_dec, t_dec, _ = x_sample.shape
    row = lambda v: v[0].reshape(1, -1)

    wb, cw, pw_re, pw_im = _s5_tables(ssm_a_re[0], ssm_a_im[0], ssm_log_dt[0], ssm_b_re[0], ssm_b_im[0],
                                      ssm_c_re[0], ssm_c_im[0], max(PROMPT_M, t_dec))
    w_in_tiles = w_in[0].astype(BF16).reshape(D_MODEL // PROJ_CHUNK, PROJ_CHUNK, D_IN // PROJ_CHUNK,
                                              PROJ_CHUNK).transpose(2, 0, 1, 3)
    shared = (w_in_tiles, wb, cw)
    lane_blocks = lambda a: a.reshape(a.shape[0], -1, LANE).transpose(1, 0, 2)
    rest = (row(ssm_d), glu_w[0].astype(BF16), row(glu_b), lane_blocks(conv_w[0]),
            lane_blocks(row(conv_b)), row(conv_ln_g),
            row(conv_ln_b), w_out[0].astype(BF16))
    per_row = lambda t, n_steps, S: jnp.repeat(t[:n_steps], S, axis=0).astype(BF16)
    mixer_weights = lambda n_steps, S: shared + (
        pw_re[:n_steps], pw_im[:n_steps], per_row(pw_re, n_steps, S), per_row(pw_im, n_steps, S)) + rest

    post_weights = (row(ln1_g), row(ln1_b),
                    mem_w_q[0].astype(BF16), mem_w_o[0].astype(BF16), row(ln2_g), row(ln2_b),
                    mlp_w1[0].astype(BF16), row(mlp_b1), mlp_w2[0].astype(BF16), row(mlp_b2),
                    row(ln3_g), row(ln3_b))

    mk, mv = _memory_kv(mem_prompt, mem_w_k[0].astype(BF16), mem_w_v[0].astype(BF16))
    zero_h = jnp.zeros((nb, 1, N_STATE), F32)
    zero_conv = jnp.zeros((nb, HALO, D_CONV), x_prompt.dtype)
    x1p, p_re, p_im, p_conv = _mixer(x_prompt, zero_h, zero_h, zero_conv, mixer_weights(PROMPT_M, PROMPT_S),
                                     S=PROMPT_S, m=PROMPT_M, chain=True)
    yp = _attn_mlp_prompt(x_prompt.reshape(nb * t_prompt, D_MODEL), x1p.reshape(nb * t_prompt, D_MODEL),
                          mk, mv, post_weights, t_prompt)
    yp = yp.reshape(nb, t_prompt, D_MODEL)

    cache_sj = cache_conv[0].transpose(1, 0, 2).reshape(1, HALO * n_dec, D_CONV)
    x1s, s_re, s_im, s_conv = _mixer(
        x_sample.reshape(1, n_dec * t_dec, D_MODEL),
        state_ssm_re[0].reshape(1, n_dec, N_STATE), state_ssm_im[0].reshape(1, n_dec, N_STATE),
        cache_sj, mixer_weights(t_dec, n_dec), S=n_dec, m=t_dec, chain=False)
    ys = _attn_mlp_streams(x_sample.reshape(n_dec * t_dec, D_MODEL), x1s.reshape(n_dec * t_dec, D_MODEL),
                           cache_mem_k[0].reshape(n_dec, N_MEM, D_MODEL),
                           cache_mem_v[0].reshape(n_dec, N_MEM, D_MODEL), post_weights, t_dec)
    ys = ys.reshape(n_dec, t_dec, D_MODEL)
    s_conv = s_conv.reshape(HALO, n_dec, D_CONV).transpose(1, 0, 2)

    st = lambda a, n: a.reshape(1, n, N_GROUPS, SSM_STATE)
    kv5 = lambda a: a.reshape(1, nb, N_MEM, MEM_HEADS, HEAD_DIM)
    return (yp, ys, st(p_re, nb), st(p_im, nb), p_conv.reshape(1, nb, HALO, D_CONV), kv5(mk), kv5(mv),
            st(s_re, n_dec), st(s_im, n_dec), s_conv.reshape(1, n_dec, HALO, D_CONV))
```
